```python
import math
import jax, jax.numpy as jnp
from jax import lax
import numpy as np

D_MODEL = 1024
BATCH = 16
SEQ = 4096
DEPTH = 1

N_HEADS = 8
HEAD_DIM = D_MODEL // (2 * N_HEADS)
V_DIM = 2 * HEAD_DIM
ATT_QK_WIDTH = N_HEADS * 2 * HEAD_DIM
ATT_V_WIDTH = N_HEADS * V_DIM
ROPE_DIM = HEAD_DIM // 4
ROPE_THETA = 500000.0
Q_BLOCK = 128
POOL_WINDOWS = (2, 4, 8, 16)
N_POOL_GROUPS = 4
POOL_GROUP_IN = 128
POOL_WIDTH = N_POOL_GROUPS * POOL_GROUP_IN
POOL_GROUP_OUT = D_MODEL // N_POOL_GROUPS
N_BRANCHES = 2
D_FF = 4 * D_MODEL
IN_WIDTH = 2 * ATT_QK_WIDTH + ATT_V_WIDTH + POOL_WIDTH + N_BRANCHES * D_MODEL
RMS_EPS = 1e-6

kernel_name = "hybrid_diffattn_pool_gated_block"


def lambda_init(layer_idx):
    return 0.8 - 0.6 * math.exp(-0.3 * layer_idx)


def rmsnorm(x, g):
    xf = x.astype(jnp.float32)
    y = xf * lax.rsqrt(jnp.mean(xf * xf, axis=-1, keepdims=True) + RMS_EPS)
    return (y * g.astype(jnp.float32)).astype(x.dtype)


def rope_partial(t, cos, sin):
    half = ROPE_DIM // 2
    t1 = t[..., :half]
    t2 = t[..., half:ROPE_DIM]
    rest = t[..., ROPE_DIM:]
    return jnp.concatenate([t1 * cos - t2 * sin, t2 * cos + t1 * sin, rest], axis=-1)


def diff_attention(q, k, v, positions, lam, subln_g, lam_init):
    B, S = q.shape[0], q.shape[1]
    inv_freq = ROPE_THETA ** (-jnp.arange(0, ROPE_DIM, 2, dtype=jnp.float32) / ROPE_DIM)
    ang = positions.astype(jnp.float32)[..., None] * inv_freq
    cos = jnp.cos(ang)[:, :, None, :].astype(q.dtype)
    sin = jnp.sin(ang)[:, :, None, :].astype(q.dtype)
    q = rope_partial(q.reshape(B, S, N_HEADS * 2, HEAD_DIM), cos, sin).reshape(B, S, N_HEADS, 2, HEAD_DIM)
    k = rope_partial(k.reshape(B, S, N_HEADS * 2, HEAD_DIM), cos, sin).reshape(B, S, N_HEADS, 2, HEAD_DIM)
    q = q * (HEAD_DIM ** -0.5)
    outs = []
    for i in range(S // Q_BLOCK):
        q0 = i * Q_BLOCK
        kend = q0 + Q_BLOCK
        qb = q[:, q0:kend]
        kb = k[:, :kend]
        vb = v[:, :kend]
        s = jnp.einsum('bqhcd,bkhcd->bhcqk', qb, kb).astype(jnp.float32)
        mask = jnp.arange(kend)[None, :] <= (q0 + jnp.arange(Q_BLOCK))[:, None]
        s = jnp.where(mask, s, -jnp.inf)
        p = jax.nn.softmax(s, axis=-1)
        w = p[:, :, 0] - lam * p[:, :, 1]
        outs.append(jnp.einsum('bhqk,bkhd->bqhd', w.astype(v.dtype), vb))
    o = jnp.concatenate(outs, axis=1)
    o = rmsnorm(o, subln_g) * (1.0 - lam_init)
    return o.reshape(B, S, ATT_V_WIDTH)


def multiscale_pool(u, w_pool, pool_scale):
    B, S = u.shape[0], u.shape[1]
    uf = u.astype(jnp.float32).reshape(B, S, N_POOL_GROUPS, POOL_GROUP_IN)
    c = lax.cumsum(uf, axis=1)
    idx = jnp.arange(S)
    parts = []
    for g, w in enumerate(POOL_WINDOWS):
        cg = c[:, :, g]
        c_prev = jnp.pad(cg, ((0, 0), (w, 0), (0, 0)))[:, :S]
        cnt = jnp.minimum(idx + 1, w).astype(jnp.float32)[None, :, None]
        parts.append((cg - c_prev) / cnt - uf[:, :, g])
    d = jnp.stack(parts, axis=2).astype(u.dtype)
    y = jnp.einsum('bsgc,gcd->bsgd', d, w_pool).reshape(B, S, D_MODEL)
    return y * pool_scale


def setup_inputs(seed: int = 0) -> dict:
    key = jax.random.key(seed)
    ks = jax.random.split(key, 18)
    f32 = jnp.float32
    x = jax.random.normal(ks[0], (BATCH, SEQ, D_MODEL), f32)
    offs = jax.random.randint(ks[1], (BATCH, 1), 0, 1024, dtype=jnp.int32)
    positions = (offs + jnp.arange(SEQ, dtype=jnp.int32)[None, :]).astype(jnp.int32)
    gain = lambda k, n: 1.0 + 0.02 * jax.random.normal(k, (DEPTH, n), f32)
    return {
        "x": x,
        "positions": positions,
        "norm_attn_g": gain(ks[2], D_MODEL),
        "w_in": jax.random.normal(ks[3], (DEPTH, D_MODEL, IN_WIDTH), f32) * D_MODEL ** -0.5,
        "lam_q1": 0.1 * jax.random.normal(ks[4], (DEPTH, HEAD_DIM), f32),
        "lam_k1": 0.1 * jax.random.normal(ks[5], (DEPTH, HEAD_DIM), f32),
        "lam_q2": 0.1 * jax.random.normal(ks[6], (DEPTH, HEAD_DIM), f32),
        "lam_k2": 0.1 * jax.random.normal(ks[7], (DEPTH, HEAD_DIM), f32),
        "subln_g": gain(ks[8], V_DIM),
        "w_pool": jax.random.normal(ks[9], (DEPTH, N_POOL_GROUPS, POOL_GROUP_IN, POOL_GROUP_OUT), f32) * POOL_GROUP_IN ** -0.5,
        "pool_scale": 1.0 + 0.1 * jax.random.normal(ks[10], (DEPTH, D_MODEL), f32),
        "w_out": jax.random.normal(ks[11], (DEPTH, D_MODEL, D_MODEL), f32) * D_MODEL ** -0.5,
        "norm_mlp_g": gain(ks[12], D_MODEL),
        "w_up": jax.random.normal(ks[13], (DEPTH, D_MODEL, D_FF), f32) * D_MODEL ** -0.5,
        "w_down": jax.random.normal(ks[14], (DEPTH, D_FF, D_MODEL), f32) * D_FF ** -0.5,
        "final_norm_g": 1.0 + 0.02 * jax.random.normal(ks[15], (D_MODEL,), f32),
    }


def reference(x, positions, norm_attn_g, w_in, lam_q1, lam_k1, lam_q2, lam_k2, subln_g,
              w_pool, pool_scale, w_out, norm_mlp_g, w_up, w_down, final_norm_g):
    B, S = x.shape[0], x.shape[1]
    splits = np.cumsum([ATT_QK_WIDTH, ATT_QK_WIDTH, ATT_V_WIDTH, POOL_WIDTH]).tolist()
    for l in range(DEPTH):
        lam_init = lambda_init(l)
        h = rmsnorm(x, norm_attn_g[l])
        u = h @ w_in[l]
        u_q, u_k, u_v, u_pool, u_gate = jnp.split(u, splits, axis=-1)
        q = u_q.reshape(B, S, N_HEADS, 2, HEAD_DIM)
        k = u_k.reshape(B, S, N_HEADS, 2, HEAD_DIM)
        v = u_v.reshape(B, S, N_HEADS, V_DIM)
        lam = (jnp.exp(jnp.sum(lam_q1[l].astype(jnp.float32) * lam_k1[l].astype(jnp.float32)))
               - jnp.exp(jnp.sum(lam_q2[l].astype(jnp.float32) * lam_k2[l].astype(jnp.float32)))
               + lam_init)
        a = diff_attention(q, k, v, positions, lam, subln_g[l], lam_init)
        p = multiscale_pool(u_pool, w_pool[l], pool_scale[l])
        gates = jax.nn.sigmoid(u_gate.reshape(B, S, N_BRANCHES, D_MODEL))
        merged = gates[:, :, 0] * a + gates[:, :, 1] * p
        x = x + (merged @ w_out[l]).astype(x.dtype)
        h2 = rmsnorm(x, norm_mlp_g[l])
        z = jnp.square(jax.nn.relu(h2 @ w_up[l]))
        x = x + (z @ w_down[l]).astype(x.dtype)
    return rmsnorm(x, final_norm_g)
```

```python
import functools
import math

import jax
import jax.numpy as jnp
from jax import lax
from jax.experimental import pallas as pl
from jax.experimental.pallas import tpu as pltpu

D_MODEL = 1024
N_HEADS = 8
HEAD_DIM = 64
V_DIM = 128
ROPE_DIM = 16
ROPE_THETA = 500000.0
POOL_WINDOWS = (2, 4, 8, 16)
POOL_GROUP_IN = 128
POOL_GROUP_OUT = 256
POOL_WIDTH = 512
D_FF = 4096
RMS_EPS = 1e-6
LANES = 128
POOL_HALO = 16

QKV_COLS = 3 * D_MODEL
GATE_COL0 = QKV_COLS + POOL_WIDTH
IN_WIDTH = GATE_COL0 + 2 * D_MODEL

VMEM_LIMIT = 56 * 1024 * 1024


def _rms(x, g):
    return x * lax.rsqrt(jnp.mean(x * x, axis=-1, keepdims=True) + RMS_EPS) * g


def _inproj_kernel(x_ref, pos_ref, invf_ref, g_ref, w_ref, qkv_ref, pool_ref, gate_ref, *, chunk):
    tm = x_ref.shape[0]
    h = _rms(x_ref[...], g_ref[...]).astype(jnp.bfloat16)

    ang = pos_ref[...].astype(jnp.float32) * invf_ref[...]
    cos, sin = jnp.cos(ang), jnp.sin(ang)
    lane = lax.broadcasted_iota(jnp.int32, (tm, LANES), 1) % HEAD_DIM
    lo = lane < ROPE_DIM // 2
    hi = (lane >= ROPE_DIM // 2) & (lane < ROPE_DIM)
    c_tab = jnp.where(lo | hi, cos, 1.0)
    s_lo = jnp.where(lo, -sin, 0.0)
    s_hi = jnp.where(hi, sin, 0.0)

    def rope(t):
        return (t * c_tab + pltpu.roll(t, LANES - ROPE_DIM // 2, 1) * s_lo
                + pltpu.roll(t, ROPE_DIM // 2, 1) * s_hi)

    for c0 in range(0, IN_WIDTH, chunk):
        r = jnp.dot(h, w_ref[:, c0:c0 + chunk], preferred_element_type=jnp.float32)
        if c0 < QKV_COLS:
            for j in range(chunk // LANES):
                slab = (c0 // LANES) + j
                t = r[:, j * LANES:(j + 1) * LANES]
                if slab < N_HEADS:
                    t = rope(t) * (HEAD_DIM ** -0.5)
                elif slab < 2 * N_HEADS:
                    t = rope(t)
                qkv_ref[0, slab] = t.astype(qkv_ref.dtype)
        elif c0 < GATE_COL0:
            pool_ref[:, c0 - QKV_COLS:c0 - QKV_COLS + chunk] = r
        else:
            gate_ref[:, c0 - GATE_COL0:c0 - GATE_COL0 + chunk] = jax.nn.sigmoid(r).astype(gate_ref.dtype)


def _inproj(x2, pos2, invf, g, w_in, B, S, tm=512, chunk=512):
    T = x2.shape[0]
    tiles_per_seq = S // tm
    const = lambda i: (0, 0)
    return pl.pallas_call(
        functools.partial(_inproj_kernel, chunk=chunk),
        grid=(T // tm,),
        in_specs=[
            pl.BlockSpec((tm, D_MODEL), lambda i: (i, 0)),
            pl.BlockSpec((tm, 1), lambda i: (i, 0)),
            pl.BlockSpec((1, LANES), const),
            pl.BlockSpec((1, D_MODEL), const),
            pl.BlockSpec((D_MODEL, IN_WIDTH), const, pipeline_mode=pl.Buffered(1)),
        ],
        out_specs=[
            pl.BlockSpec((1, 3 * N_HEADS, tm, LANES),
                         lambda i: (i // tiles_per_seq, 0, i % tiles_per_seq, 0)),
            pl.BlockSpec((tm, POOL_WIDTH), lambda i: (i, 0)),
            pl.BlockSpec((tm, 2 * D_MODEL), lambda i: (i, 0)),
        ],
        out_shape=[
            jax.ShapeDtypeStruct((B, 3 * N_HEADS, S, LANES), jnp.bfloat16),
            jax.ShapeDtypeStruct((T, POOL_WIDTH), jnp.float32),
            jax.ShapeDtypeStruct((T, 2 * D_MODEL), jnp.bfloat16),
        ],
        compiler_params=pltpu.CompilerParams(
            dimension_semantics=("arbitrary",), vmem_limit_bytes=VMEM_LIMIT),
        name="inproj",
    )(x2, pos2, invf, g, w_in)


def _attn_kernel(lq1_ref, lk1_ref, lq2_ref, lk2_ref, g_ref, q_ref, k_ref, v_ref, o_ref, *, tq, lam_init):
    S = q_ref.shape[2]
    lam = (jnp.exp(jnp.sum(lq1_ref[...] * lk1_ref[...], axis=-1, keepdims=True))
           - jnp.exp(jnp.sum(lq2_ref[...] * lk2_ref[...], axis=-1, keepdims=True))
           + lam_init)
    first_half = lax.broadcasted_iota(jnp.int32, (tq, LANES), 1) < HEAD_DIM
    row = lax.broadcasted_iota(jnp.int32, (tq, tq), 0)
    col = lax.broadcasted_iota(jnp.int32, (tq, tq), 1)
    causal = col <= row
    nt = (((1,), (1,)), ((), ()))

    def q_tile(qi, carry):
        q0 = pl.multiple_of(qi * tq, tq)
        q = q_ref[0, 0, pl.ds(q0, tq), :]
        zero = jnp.zeros_like(q)
        qs = (jnp.where(first_half, q, zero), jnp.where(first_half, zero, q))

        def block(j, state, masked):
            k0 = pl.multiple_of(j * tq, tq)
            k = k_ref[0, 0, pl.ds(k0, tq), :]
            v = v_ref[0, 0, pl.ds(k0, tq), :]
            new = []
            for c in range(2):
                m_old, l_old, acc_old = state[c]
                s = lax.dot_general(qs[c], k, nt, preferred_element_type=jnp.float32)
                if masked:
                    s = jnp.where(causal, s, -jnp.inf)
                m_new = jnp.maximum(m_old, jnp.max(s, axis=-1, keepdims=True))
                alpha = jnp.exp(m_old - m_new)
                p = jnp.exp(s - m_new)
                l_new = alpha * l_old + jnp.sum(p, axis=-1, keepdims=True)
                acc_new = alpha * acc_old + jnp.dot(p.astype(v.dtype), v, preferred_element_type=jnp.float32)
                new.append((m_new, l_new, acc_new))
            return tuple(new)

        init = tuple((jnp.full((tq, 1), -jnp.inf, jnp.float32),
                      jnp.zeros((tq, 1), jnp.float32),
                      jnp.zeros((tq, V_DIM), jnp.float32)) for _ in range(2))
        state = lax.fori_loop(0, qi, functools.partial(block, masked=False), init)
        (_, l1, acc1), (_, l2, acc2) = block(qi, state, masked=True)
        o = acc1 / l1 - lam * (acc2 / l2)
        o = _rms(o, g_ref[...]) * (1.0 - lam_init)
        o_ref[0, pl.ds(q0, tq), :] = o.astype(o_ref.dtype)
        return carry

    lax.fori_loop(0, S // tq, q_tile, 0)


def _attention(qkv, lq1, lk1, lq2, lk2, subln_g, lam_init, tq=256):
    B, _, S, _ = qkv.shape
    vec = lambda n: pl.BlockSpec((1, n), lambda b, h: (0, 0))
    slab = lambda off: pl.BlockSpec((1, 1, S, LANES), lambda b, h: (b, off + h, 0, 0))
    return pl.pallas_call(
        functools.partial(_attn_kernel, tq=tq, lam_init=lam_init),
        grid=(B, N_HEADS),
        in_specs=[vec(HEAD_DIM), vec(HEAD_DIM), vec(HEAD_DIM), vec(HEAD_DIM), vec(V_DIM),
                  slab(0), slab(N_HEADS), slab(2 * N_HEADS)],
        out_specs=pl.BlockSpec((1, S, V_DIM), lambda b, h: (b, 0, h)),
        out_shape=jax.ShapeDtypeStruct((B, S, N_HEADS * V_DIM), jnp.bfloat16),
        compiler_params=pltpu.CompilerParams(
            dimension_semantics=("arbitrary", "arbitrary"), vmem_limit_bytes=VMEM_LIMIT),
        name="diff_attn",
    )(lq1, lk1, lq2, lk2, subln_g, qkv, qkv, qkv)


def _mix_mlp_kernel(x_ref, a_ref, gate_ref, up_ref, uprev_ref, wpool_ref, pscale_ref, wout_ref,
                    g2_ref, wup_ref, wdown_ref, gf_ref, o_ref, ext_ref, *, tiles_per_seq, ff_chunk):
    tm = x_ref.shape[0]
    seq_tile = pl.program_id(0) % tiles_per_seq

    prev = uprev_ref[...]
    ext_ref[0:POOL_HALO, :] = jnp.where(seq_tile == 0, jnp.zeros_like(prev), prev)
    ext_ref[POOL_HALO:, :] = up_ref[...]
    s_idx = seq_tile * tm + lax.broadcasted_iota(jnp.int32, (tm, 1), 0)
    ys = []
    for g, w in enumerate(POOL_WINDOWS):
        cols = slice(g * POOL_GROUP_IN, (g + 1) * POOL_GROUP_IN)
        u = ext_ref[POOL_HALO:, cols]
        win = u
        for j in range(1, w):
            win = win + ext_ref[POOL_HALO - j:POOL_HALO - j + tm, cols]
        cnt = jnp.minimum(s_idx + 1, w).astype(jnp.float32)
        d = win / cnt - u
        ys.append(jnp.dot(d.astype(jnp.bfloat16), wpool_ref[g], preferred_element_type=jnp.float32))
    p = jnp.concatenate(ys, axis=-1) * pscale_ref[...]

    merged = (gate_ref[:, :D_MODEL].astype(jnp.float32) * a_ref[...].astype(jnp.float32)
              + gate_ref[:, D_MODEL:].astype(jnp.float32) * p)
    x1 = x_ref[...] + jnp.dot(merged.astype(jnp.bfloat16), wout_ref[...], preferred_element_type=jnp.float32)

    h2 = _rms(x1, g2_ref[...]).astype(jnp.bfloat16)
    acc = jnp.zeros((tm, D_MODEL), jnp.float32)
    for c0 in range(0, D_FF, ff_chunk):
        z = jnp.dot(h2, wup_ref[:, c0:c0 + ff_chunk], preferred_element_type=jnp.float32)
        z = jnp.square(jnp.maximum(z, 0.0)).astype(jnp.bfloat16)
        acc = acc + jnp.dot(z, wdown_ref[c0:c0 + ff_chunk, :], preferred_element_type=jnp.float32)
    o_ref[...] = _rms(x1 + acc, gf_ref[...])


def _mix_mlp(x2, a2, gates, u_pool, w_pool, pool_scale, w_out, g2, w_up, w_down, gf, S, tm=512, ff_chunk=1024):
    T = x2.shape[0]
    tiles_per_seq = S // tm
    halo_blocks = tm // POOL_HALO
    const2 = lambda i: (0, 0)
    resident = functools.partial(pl.BlockSpec, pipeline_mode=pl.Buffered(1))
    return pl.pallas_call(
        functools.partial(_mix_mlp_kernel, tiles_per_seq=tiles_per_seq, ff_chunk=ff_chunk),
        grid=(T // tm,),
        in_specs=[
            pl.BlockSpec((tm, D_MODEL), lambda i: (i, 0)),
            pl.BlockSpec((tm, D_MODEL), lambda i: (i, 0)),
            pl.BlockSpec((tm, 2 * D_MODEL), lambda i: (i, 0)),
            pl.BlockSpec((tm, POOL_WIDTH), lambda i: (i, 0)),
            pl.BlockSpec((POOL_HALO, POOL_WIDTH), lambda i: (jnp.maximum(i * halo_blocks - 1, 0), 0)),
            resident((len(POOL_WINDOWS), POOL_GROUP_IN, POOL_GROUP_OUT), lambda i: (0, 0, 0)),
            pl.BlockSpec((1, D_MODEL), const2),
            resident((D_MODEL, D_MODEL), const2),
            pl.BlockSpec((1, D_MODEL), const2),
            resident((D_MODEL, D_FF), const2),
            resident((D_FF, D_MODEL), const2),
            pl.BlockSpec((1, D_MODEL), const2),
        ],
        out_specs=pl.BlockSpec((tm, D_MODEL), lambda i: (i, 0)),
        out_shape=jax.ShapeDtypeStruct((T, D_MODEL), jnp.float32),
        scratch_shapes=[pltpu.VMEM((tm + POOL_HALO, POOL_WIDTH), jnp.float32)],
        compiler_params=pltpu.CompilerParams(
            dimension_semantics=("arbitrary",), vmem_limit_bytes=VMEM_LIMIT),
        name="mix_mlp",
    )(x2, a2, gates, u_pool, u_pool, w_pool, pool_scale, w_out, g2, w_up, w_down, gf)


def kernel(x, positions, norm_attn_g, w_in, lam_q1, lam_k1, lam_q2, lam_k2, subln_g, w_pool, pool_scale,
           w_out, norm_mlp_g, w_up, w_down, final_norm_g):
    B, S, D = x.shape
    assert D == D_MODEL and w_in.shape[0] == 1, "single-layer block of width 1024"
    lam_init = 0.8 - 0.6 * math.exp(-0.3 * 0)
    bf = jnp.bfloat16
    x2 = x.reshape(B * S, D)
    pos2 = positions.reshape(B * S, 1)
    inv_freq = ROPE_THETA ** (-jnp.arange(0, ROPE_DIM, 2, dtype=jnp.float32) / ROPE_DIM)
    invf = jnp.tile(inv_freq, LANES // (ROPE_DIM // 2)).reshape(1, LANES)

    qkv, u_pool, gates = _inproj(x2, pos2, invf, norm_attn_g, w_in[0].astype(bf), B, S)
    a = _attention(qkv, lam_q1, lam_k1, lam_q2, lam_k2, subln_g, lam_init)
    out = _mix_mlp(x2, a.reshape(B * S, D), gates, u_pool, w_pool[0].astype(bf), pool_scale,
                   w_out[0].astype(bf), norm_mlp_g, w_up[0].astype(bf), w_down[0].astype(bf),
                   final_norm_g.reshape(1, D), S)
    return out.reshape(B, S, D)
```

```python
import functools
import math

import jax
import jax.numpy as jnp
from jax import lax
from jax.experimental import pallas as pl
from jax.experimental.pallas import tpu as pltpu

D_MODEL = 1024
N_HEADS = 8
HEAD_DIM = 64
V_DIM = 128
ROPE_DIM = 16
ROPE_THETA = 500000.0
POOL_WINDOWS = (2, 4, 8, 16)
POOL_GROUP_IN = 128
POOL_GROUP_OUT = 256
POOL_WIDTH = 512
D_FF = 4096
RMS_EPS = 1e-6
LANES = 128
POOL_HALO = 16

QKV_COLS = 3 * D_MODEL
GATE_COL0 = QKV_COLS + POOL_WIDTH
IN_WIDTH = GATE_COL0 + 2 * D_MODEL

TOKEN_TILE = 512
ATTN_TILE = 512
STRIP = 256
HEADS_PER_STEP = 2
Q_SCALE = HEAD_DIM ** -0.5 * math.log2(math.e)
VMEM_LIMIT = 56 * 1024 * 1024


def _rms(x, g):
    return x * lax.rsqrt(jnp.mean(x * x, axis=-1, keepdims=True) + RMS_EPS) * g


def _inproj_kernel(x_ref, pos_ref, invf_ref, g_ref, w_ref, qt_ref, k_ref, vt_ref, pool_ref, gate_ref, *, chunk):
    tm = x_ref.shape[0]
    ta = qt_ref.shape[-1]
    h = _rms(x_ref[...], g_ref[...]).astype(jnp.bfloat16)

    ang = pos_ref[...].astype(jnp.float32) * invf_ref[...]
    cos, sin = jnp.cos(ang), jnp.sin(ang)
    lane = lax.broadcasted_iota(jnp.int32, (tm, LANES), 1) % HEAD_DIM
    lo = lane < ROPE_DIM // 2
    hi = (lane >= ROPE_DIM // 2) & (lane < ROPE_DIM)
    c_tab = jnp.where(lo | hi, cos, 1.0)
    s_lo = jnp.where(lo, -sin, 0.0)
    s_hi = jnp.where(hi, sin, 0.0)

    def rope(t):
        return (t * c_tab + pltpu.roll(t, LANES - ROPE_DIM // 2, 1) * s_lo
                + pltpu.roll(t, ROPE_DIM // 2, 1) * s_hi)

    def store_transposed(ref, head, t):
        for c in range(tm // ta):
            ref[0, head, c] = t[c * ta:(c + 1) * ta].T.astype(ref.dtype)

    for c0 in range(0, IN_WIDTH, chunk):
        r = jnp.dot(h, w_ref[:, c0:c0 + chunk], preferred_element_type=jnp.float32)
        if c0 < QKV_COLS:
            for j in range(chunk // LANES):
                slab = (c0 // LANES) + j
                t = r[:, j * LANES:(j + 1) * LANES]
                if slab < N_HEADS:
                    store_transposed(qt_ref, slab, rope(t) * Q_SCALE)
                elif slab < 2 * N_HEADS:
                    k_ref[0, slab - N_HEADS] = rope(t).astype(k_ref.dtype)
                else:
                    store_transposed(vt_ref, slab - 2 * N_HEADS, t)
        elif c0 < GATE_COL0:
            pool_ref[:, c0 - QKV_COLS:c0 - QKV_COLS + chunk] = r
        else:
            gate_ref[:, c0 - GATE_COL0:c0 - GATE_COL0 + chunk] = jax.nn.sigmoid(r).astype(gate_ref.dtype)


def _inproj(x2, pos2, invf, g, w_in, B, S, tm=TOKEN_TILE, ta=ATTN_TILE, chunk=512):
    T = x2.shape[0]
    tiles_per_seq = S // tm
    sub = tm // ta
    const = lambda i: (0, 0)
    bf = jnp.bfloat16
    transposed = pl.BlockSpec((1, N_HEADS, sub, LANES, ta),
                              lambda i: (i // tiles_per_seq, 0, i % tiles_per_seq, 0, 0))
    return pl.pallas_call(
        functools.partial(_inproj_kernel, chunk=chunk),
        grid=(T // tm,),
        in_specs=[
            pl.BlockSpec((tm, D_MODEL), lambda i: (i, 0)),
            pl.BlockSpec((tm, 1), lambda i: (i, 0)),
            pl.BlockSpec((1, LANES), const),
            pl.BlockSpec((1, D_MODEL), const),
            pl.BlockSpec((D_MODEL, IN_WIDTH), const, pipeline_mode=pl.Buffered(1)),
        ],
        out_specs=[
            transposed,
            pl.BlockSpec((1, N_HEADS, tm, LANES), lambda i: (i // tiles_per_seq, 0, i % tiles_per_seq, 0)),
            transposed,
            pl.BlockSpec((tm, POOL_WIDTH), lambda i: (i, 0)),
            pl.BlockSpec((tm, 2 * D_MODEL), lambda i: (i, 0)),
        ],
        out_shape=[
            jax.ShapeDtypeStruct((B, N_HEADS, S // ta, LANES, ta), bf),
            jax.ShapeDtypeStruct((B, N_HEADS, S, LANES), bf),
            jax.ShapeDtypeStruct((B, N_HEADS, S // ta, LANES, ta), bf),
            jax.ShapeDtypeStruct((T, POOL_WIDTH), jnp.float32),
            jax.ShapeDtypeStruct((T, 2 * D_MODEL), bf),
        ],
        compiler_params=pltpu.CompilerParams(
            dimension_semantics=("arbitrary",), vmem_limit_bytes=VMEM_LIMIT),
        name="inproj",
    )(x2, pos2, invf, g, w_in)


def _attn_kernel(lq1_ref, lk1_ref, lq2_ref, lk2_ref, g_ref, qt_ref, k_ref, vt_ref, o_ref, acc_ref, s_ref, *,
                 lam_init):
    n_heads, n_tiles, _, ta = qt_ref.shape[1:]
    lam = (jnp.exp(jnp.sum(lq1_ref[...] * lk1_ref[...], axis=-1, keepdims=True))
           - jnp.exp(jnp.sum(lq2_ref[...] * lk2_ref[...], axis=-1, keepdims=True))
           + lam_init)
    first_map = lax.broadcasted_iota(jnp.int32, (LANES, STRIP), 0) < HEAD_DIM

    strips_per_map = ta // STRIP
    items = [(h, c) for c in range(2 * strips_per_map) for h in range(n_heads)]
    neg_inf = jnp.full((1, STRIP), -jnp.inf, jnp.float32)

    def scores(qi, j, h, c):
        q0 = (c % strips_per_map) * STRIP
        qt = qt_ref[0, h, qi, :, q0:q0 + STRIP]
        zero = jnp.zeros_like(qt)
        w = jnp.where(first_map, qt, zero) if c < strips_per_map else jnp.where(first_map, zero, qt)
        k0 = pl.multiple_of(j * ta, ta)
        s = jnp.dot(k_ref[0, h, pl.ds(k0, ta), :], w, preferred_element_type=jnp.float32)
        s_ref[h, :, c * STRIP:(c + 1) * STRIP] = s
        return jnp.max(s, axis=0, keepdims=True)

    def accumulate(j, h, c, s_max, m_old, l_old, masked):
        cols = slice(c * STRIP, (c + 1) * STRIP)
        s = s_ref[h, :, cols]
        if masked:
            key = lax.broadcasted_iota(jnp.int32, (ta, STRIP), 0)
            qry = lax.broadcasted_iota(jnp.int32, (ta, STRIP), 1) + (c % strips_per_map) * STRIP
            s = jnp.where(key <= qry, s, -jnp.inf)
            s_max = jnp.max(s, axis=0, keepdims=True)
        m_new = jnp.maximum(m_old, s_max)
        alpha = jnp.exp2(m_old - m_new)
        p = jnp.exp2(s - m_new)
        l_new = alpha * l_old + jnp.sum(p, axis=0, keepdims=True)
        pv = jnp.dot(vt_ref[0, h, j], p.astype(jnp.bfloat16), preferred_element_type=jnp.float32)
        acc_ref[h, :, cols] = alpha * acc_ref[h, :, cols] + pv
        return m_new, l_new

    def step(qi, j, qi_next, j_next, state, masked):
        new = []
        for (h, c), (s_max, m, l) in zip(items, state):
            m, l = accumulate(j, h, c, s_max, m, l, masked)
            new.append((scores(qi_next, j_next, h, c), m, l))
        return tuple(new)

    def q_tile(qi, s_maxes):
        acc_ref[...] = jnp.zeros_like(acc_ref)
        init = tuple((s_max, neg_inf, jnp.zeros((1, STRIP), jnp.float32)) for s_max in s_maxes)
        state = lax.fori_loop(0, qi, lambda j, st: step(qi, j, qi, j + 1, st, False), init)
        state = step(qi, qi, jnp.minimum(qi + 1, n_tiles - 1), 0, state, True)

        l = {item: st[2] for item, st in zip(items, state)}
        q0 = pl.multiple_of(qi * ta, ta)
        for h in range(n_heads):
            for c in range(strips_per_map):
                c1, c2 = slice(c * STRIP, (c + 1) * STRIP), slice(ta + c * STRIP, ta + (c + 1) * STRIP)
                o = acc_ref[h, :, c1] / l[h, c] - lam * (acc_ref[h, :, c2] / l[h, strips_per_map + c])
                o = o * lax.rsqrt(jnp.mean(o * o, axis=0, keepdims=True) + RMS_EPS) * g_ref[...]
                o = o * (1.0 - lam_init)
                o_ref[0, pl.ds(q0 + c * STRIP, STRIP), h * V_DIM:(h + 1) * V_DIM] = o.T.astype(o_ref.dtype)
        return tuple(st[0] for st in state)

    lax.fori_loop(0, n_tiles, q_tile, tuple(scores(0, 0, h, c) for h, c in items))


def _attention(qt, k, vt, lq1, lk1, lq2, lk2, subln_g, lam_init):
    B, H, n_tiles, _, ta = qt.shape
    S = n_tiles * ta
    hs = HEADS_PER_STEP
    vec = lambda n: pl.BlockSpec((1, n), lambda b, h: (0, 0))
    transposed = pl.BlockSpec((1, hs, n_tiles, LANES, ta), lambda b, h: (b, h, 0, 0, 0))
    return pl.pallas_call(
        functools.partial(_attn_kernel, lam_init=lam_init),
        grid=(B, H // hs),
        in_specs=[vec(HEAD_DIM), vec(HEAD_DIM), vec(HEAD_DIM), vec(HEAD_DIM),
                  pl.BlockSpec((V_DIM, 1), lambda b, h: (0, 0)),
                  transposed,
                  pl.BlockSpec((1, hs, S, LANES), lambda b, h: (b, h, 0, 0)),
                  transposed],
        out_specs=pl.BlockSpec((1, S, hs * V_DIM), lambda b, h: (b, 0, h)),
        out_shape=jax.ShapeDtypeStruct((B, S, H * V_DIM), jnp.bfloat16),
        scratch_shapes=[pltpu.VMEM((hs, V_DIM, 2 * ta), jnp.float32),
                        pltpu.VMEM((hs, ta, 2 * ta), jnp.float32)],
        compiler_params=pltpu.CompilerParams(
            dimension_semantics=("arbitrary", "arbitrary"), vmem_limit_bytes=VMEM_LIMIT),
        name="diff_attn",
    )(lq1, lk1, lq2, lk2, subln_g.reshape(V_DIM, 1), qt, k, vt)


def _mix_mlp_kernel(x_ref, a_ref, gate_ref, up_ref, uprev_ref, wpool_ref, pscale_ref, wout_ref,
                    g2_ref, wup_ref, wdown_ref, gf_ref, o_ref, ext_ref, *, tiles_per_seq, ff_chunk):
    tm = x_ref.shape[0]
    seq_tile = pl.program_id(0) % tiles_per_seq

    prev = uprev_ref[...]
    ext_ref[0:POOL_HALO, :] = jnp.where(seq_tile == 0, jnp.zeros_like(prev), prev)
    ext_ref[POOL_HALO:, :] = up_ref[...]
    s_idx = seq_tile * tm + lax.broadcasted_iota(jnp.int32, (tm, 1), 0)
    ys = []
    for g, w in enumerate(POOL_WINDOWS):
        cols = slice(g * POOL_GROUP_IN, (g + 1) * POOL_GROUP_IN)
        u = ext_ref[POOL_HALO:, cols]
        win = u
        for j in range(1, w):
            win = win + ext_ref[POOL_HALO - j:POOL_HALO - j + tm, cols]
        cnt = jnp.minimum(s_idx + 1, w).astype(jnp.float32)
        d = win / cnt - u
        ys.append(jnp.dot(d.astype(jnp.bfloat16), wpool_ref[g], preferred_element_type=jnp.float32))
    p = jnp.concatenate(ys, axis=-1) * pscale_ref[...]

    merged = (gate_ref[:, :D_MODEL].astype(jnp.float32) * a_ref[...].astype(jnp.float32)
              + gate_ref[:, D_MODEL:].astype(jnp.float32) * p)
    x1 = x_ref[...] + jnp.dot(merged.astype(jnp.bfloat16), wout_ref[...], preferred_element_type=jnp.float32)

    h2 = _rms(x1, g2_ref[...]).astype(jnp.bfloat16)
    acc = jnp.zeros((tm, D_MODEL), jnp.float32)
    for c0 in range(0, D_FF, ff_chunk):
        z = jnp.dot(h2, wup_ref[:, c0:c0 + ff_chunk], preferred_element_type=jnp.float32)
        z = jnp.square(jnp.maximum(z, 0.0)).astype(jnp.bfloat16)
        acc = acc + jnp.dot(z, wdown_ref[c0:c0 + ff_chunk, :], preferred_element_type=jnp.float32)
    o_ref[...] = _rms(x1 + acc, gf_ref[...])


def _mix_mlp(x2, a2, gates, u_pool, w_pool, pool_scale, w_out, g2, w_up, w_down, gf, S, tm=TOKEN_TILE, ff_chunk=1024):
    T = x2.shape[0]
    tiles_per_seq = S // tm
    halo_blocks = tm // POOL_HALO
    const2 = lambda i: (0, 0)
    resident = functools.partial(pl.BlockSpec, pipeline_mode=pl.Buffered(1))
    return pl.pallas_call(
        functools.partial(_mix_mlp_kernel, tiles_per_seq=tiles_per_seq, ff_chunk=ff_chunk),
        grid=(T // tm,),
        in_specs=[
            pl.BlockSpec((tm, D_MODEL), lambda i: (i, 0)),
            pl.BlockSpec((tm, D_MODEL), lambda i: (i, 0)),
            pl.BlockSpec((tm, 2 * D_MODEL), lambda i: (i, 0)),
            pl.BlockSpec((tm, POOL_WIDTH), lambda i: (i, 0)),
            pl.BlockSpec((POOL_HALO, POOL_WIDTH), lambda i: (jnp.maximum(i * halo_blocks - 1, 0), 0)),
            resident((len(POOL_WINDOWS), POOL_GROUP_IN, POOL_GROUP_OUT), lambda i: (0, 0, 0)),
            pl.BlockSpec((1, D_MODEL), const2),
            resident((D_MODEL, D_MODEL), const2),
            pl.BlockSpec((1, D_MODEL), const2),
            resident((D_MODEL, D_FF), const2),
            resident((D_FF, D_MODEL), const2),
            pl.BlockSpec((1, D_MODEL), const2),
        ],
        out_specs=pl.BlockSpec((tm, D_MODEL), lambda i: (i, 0)),
        out_shape=jax.ShapeDtypeStruct((T, D_MODEL), jnp.float32),
        scratch_shapes=[pltpu.VMEM((tm + POOL_HALO, POOL_WIDTH), jnp.float32)],
        compiler_params=pltpu.CompilerParams(
            dimension_semantics=("arbitrary",), vmem_limit_bytes=VMEM_LIMIT),
        name="mix_mlp",
    )(x2, a2, gates, u_pool, u_pool, w_pool, pool_scale, w_out, g2, w_up, w_down, gf)


def kernel(x, positions, norm_attn_g, w_in, lam_q1, lam_k1, lam_q2, lam_k2, subln_g, w_pool, pool_scale,
           w_out, norm_mlp_g, w_up, w_down, final_norm_g):
    B, S, D = x.shape
    assert D == D_MODEL and w_in.shape[0] == 1, "single-layer block of width 1024"
    lam_init = 0.8 - 0.6 * math.exp(-0.3 * 0)
    bf = jnp.bfloat16
    x2 = x.reshape(B * S, D)
    pos2 = positions.reshape(B * S, 1)
    inv_freq = ROPE_THETA ** (-jnp.arange(0, ROPE_DIM, 2, dtype=jnp.float32) / ROPE_DIM)
    invf = jnp.tile(inv_freq, LANES // (ROPE_DIM // 2)).reshape(1, LANES)

    qt, k, vt, u_pool, gates = _inproj(x2, pos2, invf, norm_attn_g, w_in[0].astype(bf), B, S)
    a = _attention(qt, k, vt, lam_q1, lam_k1, lam_q2, lam_k2, subln_g, lam_init)
    out = _mix_mlp(x2, a.reshape(B * S, D), gates, u_pool, w_pool[0].astype(bf), pool_scale,
                   w_out[0].astype(bf), norm_mlp_g, w_up[0].astype(bf), w_down[0].astype(bf),
                   final_norm_g.reshape(1, D), S)
    return out.reshape(B, S, D)
```

```python
import functools
import math

import jax
import jax.numpy as jnp
from jax import lax
from jax.experimental import pallas as pl
from jax.experimental.pallas import tpu as pltpu

D_MODEL = 1024
N_HEADS = 8
HEAD_DIM = 64
V_DIM = 128
ROPE_DIM = 16
ROPE_THETA = 500000.0
POOL_WINDOWS = (2, 4, 8, 16)
POOL_GROUP_IN = 128
POOL_GROUP_OUT = 256
POOL_WIDTH = 512
D_FF = 4096
RMS_EPS = 1e-6
LANES = 128
POOL_HALO = 16

QKV_COLS = 3 * D_MODEL
GATE_COL0 = QKV_COLS + POOL_WIDTH
IN_WIDTH = GATE_COL0 + 2 * D_MODEL

TOKEN_TILE = 512
ATTN_TILE = 512
STRIP = 256
HEADS_PER_STEP = 4
Q_SCALE = HEAD_DIM ** -0.5 * math.log2(math.e)
VMEM_LIMIT = 56 * 1024 * 1024


def _rms(x, g):
    return x * lax.rsqrt(jnp.mean(x * x, axis=-1, keepdims=True) + RMS_EPS) * g


def _inproj_kernel(x_ref, pos_ref, invf_ref, g_ref, w_ref, qt_ref, k_ref, vt_ref, pool_ref, gate_ref, *, chunk):
    tm = x_ref.shape[0]
    ta = qt_ref.shape[-1]
    h = _rms(x_ref[...], g_ref[...]).astype(jnp.bfloat16)

    ang = pos_ref[...].astype(jnp.float32) * invf_ref[...]
    cos, sin = jnp.cos(ang), jnp.sin(ang)
    lane = lax.broadcasted_iota(jnp.int32, (tm, LANES), 1) % HEAD_DIM
    lo = lane < ROPE_DIM // 2
    hi = (lane >= ROPE_DIM // 2) & (lane < ROPE_DIM)
    c_tab = jnp.where(lo | hi, cos, 1.0)
    s_lo = jnp.where(lo, -sin, 0.0)
    s_hi = jnp.where(hi, sin, 0.0)

    def rope(t):
        return (t * c_tab + pltpu.roll(t, LANES - ROPE_DIM // 2, 1) * s_lo
                + pltpu.roll(t, ROPE_DIM // 2, 1) * s_hi)

    def store_transposed(ref, head, t):
        for c in range(tm // ta):
            ref[0, head, c] = t[c * ta:(c + 1) * ta].T.astype(ref.dtype)

    for c0 in range(0, IN_WIDTH, chunk):
        r = jnp.dot(h, w_ref[:, c0:c0 + chunk], preferred_element_type=jnp.float32)
        if c0 < QKV_COLS:
            for j in range(chunk // LANES):
                slab = (c0 // LANES) + j
                t = r[:, j * LANES:(j + 1) * LANES]
                if slab < N_HEADS:
                    store_transposed(qt_ref, slab, rope(t) * Q_SCALE)
                elif slab < 2 * N_HEADS:
                    k_ref[0, slab - N_HEADS] = rope(t).astype(k_ref.dtype)
                else:
                    store_transposed(vt_ref, slab - 2 * N_HEADS, t)
        elif c0 < GATE_COL0:
            pool_ref[:, c0 - QKV_COLS:c0 - QKV_COLS + chunk] = r
        else:
            gate_ref[:, c0 - GATE_COL0:c0 - GATE_COL0 + chunk] = jax.nn.sigmoid(r).astype(gate_ref.dtype)


def _inproj(x2, pos2, invf, g, w_in, B, S, tm=TOKEN_TILE, ta=ATTN_TILE, chunk=512):
    T = x2.shape[0]
    tiles_per_seq = S // tm
    sub = tm // ta
    const = lambda i: (0, 0)
    bf = jnp.bfloat16
    transposed = pl.BlockSpec((1, N_HEADS, sub, LANES, ta),
                              lambda i: (i // tiles_per_seq, 0, i % tiles_per_seq, 0, 0))
    return pl.pallas_call(
        functools.partial(_inproj_kernel, chunk=chunk),
        grid=(T // tm,),
        in_specs=[
            pl.BlockSpec((tm, D_MODEL), lambda i: (i, 0)),
            pl.BlockSpec((tm, 1), lambda i: (i, 0)),
            pl.BlockSpec((1, LANES), const),
            pl.BlockSpec((1, D_MODEL), const),
            pl.BlockSpec((D_MODEL, IN_WIDTH), const, pipeline_mode=pl.Buffered(1)),
        ],
        out_specs=[
            transposed,
            pl.BlockSpec((1, N_HEADS, tm, LANES), lambda i: (i // tiles_per_seq, 0, i % tiles_per_seq, 0)),
            transposed,
            pl.BlockSpec((tm, POOL_WIDTH), lambda i: (i, 0)),
            pl.BlockSpec((tm, 2 * D_MODEL), lambda i: (i, 0)),
        ],
        out_shape=[
            jax.ShapeDtypeStruct((B, N_HEADS, S // ta, LANES, ta), bf),
            jax.ShapeDtypeStruct((B, N_HEADS, S, LANES), bf),
            jax.ShapeDtypeStruct((B, N_HEADS, S // ta, LANES, ta), bf),
            jax.ShapeDtypeStruct((T, POOL_WIDTH), jnp.float32),
            jax.ShapeDtypeStruct((T, 2 * D_MODEL), bf),
        ],
        compiler_params=pltpu.CompilerParams(
            dimension_semantics=("arbitrary",), vmem_limit_bytes=VMEM_LIMIT),
        name="inproj",
    )(x2, pos2, invf, g, w_in)


def _attn_kernel(lq1_ref, lk1_ref, lq2_ref, lk2_ref, g_ref, qt_ref, k_ref, vt_ref, o_ref,
                 acc_ref, s_ref, *, lam_init):
    n_heads, n_tiles, _, ta = qt_ref.shape[1:]
    lam = (jnp.exp(jnp.sum(lq1_ref[...] * lk1_ref[...], axis=-1, keepdims=True))
           - jnp.exp(jnp.sum(lq2_ref[...] * lk2_ref[...], axis=-1, keepdims=True))
           + lam_init)
    first_map = lax.broadcasted_iota(jnp.int32, (LANES, STRIP), 0) < HEAD_DIM

    strips_per_map = ta // STRIP
    n_strips = 2 * strips_per_map
    items = [(h, c) for c in range(n_strips) for h in range(n_heads)]
    neg_inf = jnp.full((1, STRIP), -jnp.inf, jnp.float32)

    def scores(qi, j, h, c):
        q0 = (c % strips_per_map) * STRIP
        qt = qt_ref[0, h, qi, :, q0:q0 + STRIP]
        zero = jnp.zeros_like(qt)
        w = jnp.where(first_map, qt, zero) if c < strips_per_map else jnp.where(first_map, zero, qt)
        k0 = pl.multiple_of(j * ta, ta)
        s = jnp.dot(k_ref[0, h, pl.ds(k0, ta), :], w, preferred_element_type=jnp.float32)
        s_ref[h, c] = s
        return jnp.max(s, axis=0, keepdims=True)

    def accumulate(j, h, c, s_max, m_old, l_old, diagonal):
        if diagonal:
            q0 = (c % strips_per_map) * STRIP
            n_keys = q0 + STRIP
            causal = (lax.broadcasted_iota(jnp.int32, (STRIP, STRIP), 0)
                      <= lax.broadcasted_iota(jnp.int32, (STRIP, STRIP), 1))
            s = jnp.concatenate(
                [jnp.where(causal, s_ref[h, c, kb:kb + STRIP, :], -jnp.inf) if kb == q0
                 else s_ref[h, c, kb:kb + STRIP, :] for kb in range(0, n_keys, STRIP)], axis=0)
            s_max = jnp.max(s, axis=0, keepdims=True)
            vt = vt_ref[0, h, j, :, 0:n_keys]
        else:
            s = s_ref[h, c]
            vt = vt_ref[0, h, j]
        m_new = jnp.maximum(m_old, s_max)
        alpha = jnp.exp2(m_old - m_new)
        p = jnp.exp2(s - m_new)
        l_new = alpha * l_old + jnp.sum(p, axis=0, keepdims=True)
        pv = jnp.dot(vt, p.astype(jnp.bfloat16), preferred_element_type=jnp.float32)
        acc_ref[h, c] = alpha * acc_ref[h, c] + pv
        return m_new, l_new

    def finish(qi, h, l):
        q0 = pl.multiple_of(qi * ta, ta)
        for c in range(strips_per_map):
            c2 = strips_per_map + c
            o = acc_ref[h, c] * (1.0 / l[c]) - acc_ref[h, c2] * (lam / l[c2])
            scale = lax.rsqrt(jnp.mean(o * o, axis=0, keepdims=True) + RMS_EPS) * (1.0 - lam_init)
            o = o * scale * g_ref[...]
            o_ref[0, pl.ds(q0 + c * STRIP, STRIP), h * V_DIM:(h + 1) * V_DIM] = o.T.astype(o_ref.dtype)

    def step(qi, j, qi_next, j_next, state, diagonal):
        new, row_sums = [], {}
        for (h, c), (s_max, m, l) in zip(items, state):
            m, l = accumulate(j, h, c, s_max, m, l, diagonal)
            new.append((scores(qi_next, j_next, h, c), m, l))
            row_sums[h, c] = l
            if diagonal and all((h, cc) in row_sums for cc in range(n_strips)):
                finish(qi, h, [row_sums[h, cc] for cc in range(n_strips)])
        return tuple(new)

    def q_tile(qi, s_maxes):
        acc_ref[...] = jnp.zeros_like(acc_ref)
        init = tuple((s_max, neg_inf, jnp.zeros((1, STRIP), jnp.float32)) for s_max in s_maxes)
        state = lax.fori_loop(0, qi, lambda j, st: step(qi, j, qi, j + 1, st, False), init)
        state = step(qi, qi, jnp.minimum(qi + 1, n_tiles - 1), 0, state, True)
        return tuple(st[0] for st in state)

    lax.fori_loop(0, n_tiles, q_tile, tuple(scores(0, 0, h, c) for h, c in items))


def _attention(qt, k, vt, lq1, lk1, lq2, lk2, subln_g, lam_init):
    B, H, n_tiles, _, ta = qt.shape
    S = n_tiles * ta
    hs = HEADS_PER_STEP
    vec = lambda n: pl.BlockSpec((1, n), lambda b, h: (0, 0))
    transposed = pl.BlockSpec((1, hs, n_tiles, LANES, ta), lambda b, h: (b, h, 0, 0, 0))
    return pl.pallas_call(
        functools.partial(_attn_kernel, lam_init=lam_init),
        grid=(B, H // hs),
        in_specs=[vec(HEAD_DIM), vec(HEAD_DIM), vec(HEAD_DIM), vec(HEAD_DIM),
                  pl.BlockSpec((V_DIM, 1), lambda b, h: (0, 0)),
                  transposed,
                  pl.BlockSpec((1, hs, S, LANES), lambda b, h: (b, h, 0, 0)),
                  transposed],
        out_specs=pl.BlockSpec((1, S, hs * V_DIM), lambda b, h: (b, 0, h)),
        out_shape=jax.ShapeDtypeStruct((B, S, H * V_DIM), jnp.bfloat16),
        scratch_shapes=[pltpu.VMEM((hs, 2 * ta // STRIP, V_DIM, STRIP), jnp.float32),
                        pltpu.VMEM((hs, 2 * ta // STRIP, ta, STRIP), jnp.float32)],
        compiler_params=pltpu.CompilerParams(
            dimension_semantics=("arbitrary", "arbitrary"), vmem_limit_bytes=VMEM_LIMIT),
        name="diff_attn",
    )(lq1, lk1, lq2, lk2, subln_g.reshape(V_DIM, 1), qt, k, vt)


def _mix_mlp_kernel(x_ref, a_ref, gate_ref, up_ref, uprev_ref, wpool_ref, pscale_ref, wout_ref,
                    g2_ref, wup_ref, wdown_ref, gf_ref, o_ref, ext_ref, *, tiles_per_seq, ff_chunk):
    tm = x_ref.shape[0]
    seq_tile = pl.program_id(0) % tiles_per_seq

    prev = uprev_ref[...]
    ext_ref[0:POOL_HALO, :] = jnp.where(seq_tile == 0, jnp.zeros_like(prev), prev)
    ext_ref[POOL_HALO:, :] = up_ref[...]
    s_idx = seq_tile * tm + lax.broadcasted_iota(jnp.int32, (tm, 1), 0)
    ys = []
    for g, w in enumerate(POOL_WINDOWS):
        cols = slice(g * POOL_GROUP_IN, (g + 1) * POOL_GROUP_IN)
        u = ext_ref[POOL_HALO:, cols]
        win = u
        for j in range(1, w):
            win = win + ext_ref[POOL_HALO - j:POOL_HALO - j + tm, cols]
        cnt = jnp.minimum(s_idx + 1, w).astype(jnp.float32)
        d = win / cnt - u
        ys.append(jnp.dot(d.astype(jnp.bfloat16), wpool_ref[g], preferred_element_type=jnp.float32))
    p = jnp.concatenate(ys, axis=-1) * pscale_ref[...]

    merged = (gate_ref[:, :D_MODEL].astype(jnp.float32) * a_ref[...].astype(jnp.float32)
              + gate_ref[:, D_MODEL:].astype(jnp.float32) * p)
    x1 = x_ref[...] + jnp.dot(merged.astype(jnp.bfloat16), wout_ref[...], preferred_element_type=jnp.float32)

    h2 = _rms(x1, g2_ref[...]).astype(jnp.bfloat16)
    acc = jnp.zeros((tm, D_MODEL), jnp.float32)
    for c0 in range(0, D_FF, ff_chunk):
        z = jnp.dot(h2, wup_ref[:, c0:c0 + ff_chunk], preferred_element_type=jnp.float32)
        z = jnp.square(jnp.maximum(z, 0.0)).astype(jnp.bfloat16)
        acc = acc + jnp.dot(z, wdown_ref[c0:c0 + ff_chunk, :], preferred_element_type=jnp.float32)
    o_ref[...] = _rms(x1 + acc, gf_ref[...])


def _mix_mlp(x2, a2, gates, u_pool, w_pool, pool_scale, w_out, g2, w_up, w_down, gf, S, tm=TOKEN_TILE, ff_chunk=1024):
    T = x2.shape[0]
    tiles_per_seq = S // tm
    halo_blocks = tm // POOL_HALO
    const2 = lambda i: (0, 0)
    resident = functools.partial(pl.BlockSpec, pipeline_mode=pl.Buffered(1))
    return pl.pallas_call(
        functools.partial(_mix_mlp_kernel, tiles_per_seq=tiles_per_seq, ff_chunk=ff_chunk),
        grid=(T // tm,),
        in_specs=[
            pl.BlockSpec((tm, D_MODEL), lambda i: (i, 0)),
            pl.BlockSpec((tm, D_MODEL), lambda i: (i, 0)),
            pl.BlockSpec((tm, 2 * D_MODEL), lambda i: (i, 0)),
            pl.BlockSpec((tm, POOL_WIDTH), lambda i: (i, 0)),
            pl.BlockSpec((POOL_HALO, POOL_WIDTH), lambda i: (jnp.maximum(i * halo_blocks - 1, 0), 0)),
            resident((len(POOL_WINDOWS), POOL_GROUP_IN, POOL_GROUP_OUT), lambda i: (0, 0, 0)),
            pl.BlockSpec((1, D_MODEL), const2),
            resident((D_MODEL, D_MODEL), const2),
            pl.BlockSpec((1, D_MODEL), const2),
            resident((D_MODEL, D_FF), const2),
            resident((D_FF, D_MODEL), const2),
            pl.BlockSpec((1, D_MODEL), const2),
        ],
        out_specs=pl.BlockSpec((tm, D_MODEL), lambda i: (i, 0)),
        out_shape=jax.ShapeDtypeStruct((T, D_MODEL), jnp.float32),
        scratch_shapes=[pltpu.VMEM((tm + POOL_HALO, POOL_WIDTH), jnp.float32)],
        compiler_params=pltpu.CompilerParams(
            dimension_semantics=("arbitrary",), vmem_limit_bytes=VMEM_LIMIT),
        name="mix_mlp",
    )(x2, a2, gates, u_pool, u_pool, w_pool, pool_scale, w_out, g2, w_up, w_down, gf)


def kernel(x, positions, norm_attn_g, w_in, lam_q1, lam_k1, lam_q2, lam_k2, subln_g, w_pool, pool_scale,
           w_out, norm_mlp_g, w_up, w_down, final_norm_g):
    B, S, D = x.shape
    assert D == D_MODEL and w_in.shape[0] == 1, "single-layer block of width 1024"
    lam_init = 0.8 - 0.6 * math.exp(-0.3 * 0)
    bf = jnp.bfloat16
    x2 = x.reshape(B * S, D)
    pos2 = positions.reshape(B * S, 1)
    inv_freq = ROPE_THETA ** (-jnp.arange(0, ROPE_DIM, 2, dtype=jnp.float32) / ROPE_DIM)
    invf = jnp.tile(inv_freq, LANES // (ROPE_DIM // 2)).reshape(1, LANES)

    qt, k, vt, u_pool, gates = _inproj(x2, pos2, invf, norm_attn_g, w_in[0].astype(bf), B, S)
    a = _attention(qt, k, vt, lam_q1, lam_k1, lam_q2, lam_k2, subln_g, lam_init)
    out = _mix_mlp(x2, a.reshape(B * S, D), gates, u_pool, w_pool[0].astype(bf), pool_scale,
                   w_out[0].astype(bf), norm_mlp_g, w_up[0].astype(bf), w_down[0].astype(bf),
                   final_norm_g.reshape(1, D), S)
    return out.reshape(B, S, D)
```

```python
import functools
import math

import jax
import jax.numpy as jnp
from jax import lax
from jax.experimental import pallas as pl
from jax.experimental.pallas import tpu as pltpu

D_MODEL = 1024
N_HEADS = 8
HEAD_DIM = 64
V_DIM = 128
ROPE_DIM = 16
ROPE_THETA = 500000.0
POOL_WINDOWS = (2, 4, 8, 16)
POOL_GROUP_IN = 128
POOL_GROUP_OUT = 256
POOL_WIDTH = 512
D_FF = 4096
RMS_EPS = 1e-6
LANES = 128
POOL_HALO = 16

QKV_COLS = 3 * D_MODEL
GATE_COL0 = QKV_COLS + POOL_WIDTH
IN_WIDTH = GATE_COL0 + 2 * D_MODEL

TOKEN_TILE = 512
ATTN_TILE = 512
STRIP = 256
HEADS_PER_STEP = 4
Q_SCALE = HEAD_DIM ** -0.5 * math.log2(math.e)
VMEM_LIMIT = 56 * 1024 * 1024


def _rms(x, g):
    return x * lax.rsqrt(jnp.mean(x * x, axis=-1, keepdims=True) + RMS_EPS) * g


def _inproj_kernel(x_ref, pos_ref, invf_ref, g_ref, w_ref, qt_ref, k_ref, vt_ref, pool_ref, gate_ref, *, chunk):
    tm = x_ref.shape[0]
    ta = qt_ref.shape[-1]
    h = _rms(x_ref[...], g_ref[...]).astype(jnp.bfloat16)

    ang = pos_ref[...].astype(jnp.float32) * invf_ref[...]
    cos, sin = jnp.cos(ang), jnp.sin(ang)
    lane = lax.broadcasted_iota(jnp.int32, (tm, LANES), 1) % HEAD_DIM
    lo = lane < ROPE_DIM // 2
    hi = (lane >= ROPE_DIM // 2) & (lane < ROPE_DIM)
    c_tab = jnp.where(lo | hi, cos, 1.0)
    s_lo = jnp.where(lo, -sin, 0.0)
    s_hi = jnp.where(hi, sin, 0.0)

    def rope(t):
        return (t * c_tab + pltpu.roll(t, LANES - ROPE_DIM // 2, 1) * s_lo
                + pltpu.roll(t, ROPE_DIM // 2, 1) * s_hi)

    def store_transposed(ref, head, t):
        for c in range(tm // ta):
            ref[0, head, c] = t[c * ta:(c + 1) * ta].T.astype(ref.dtype)

    starts = list(range(0, IN_WIDTH, chunk))
    order = ([c0 for c0 in starts if c0 >= GATE_COL0] + [c0 for c0 in starts if 2 * D_MODEL <= c0 < QKV_COLS]
             + [c0 for c0 in starts if c0 < 2 * D_MODEL] + [c0 for c0 in starts if QKV_COLS <= c0 < GATE_COL0])
    for c0 in order:
        r = jnp.dot(h, w_ref[:, c0:c0 + chunk], preferred_element_type=jnp.float32)
        if c0 < QKV_COLS:
            for j in range(chunk // LANES):
                slab = (c0 // LANES) + j
                t = r[:, j * LANES:(j + 1) * LANES]
                if slab < N_HEADS:
                    store_transposed(qt_ref, slab, rope(t) * Q_SCALE)
                elif slab < 2 * N_HEADS:
                    k_ref[0, slab - N_HEADS] = rope(t).astype(k_ref.dtype)
                else:
                    store_transposed(vt_ref, slab - 2 * N_HEADS, t)
        elif c0 < GATE_COL0:
            pool_ref[:, c0 - QKV_COLS:c0 - QKV_COLS + chunk] = r
        else:
            gate_ref[:, c0 - GATE_COL0:c0 - GATE_COL0 + chunk] = jax.nn.sigmoid(r).astype(gate_ref.dtype)


def _inproj(x2, pos2, invf, g, w_in, B, S, tm=TOKEN_TILE, ta=ATTN_TILE, chunk=512):
    T = x2.shape[0]
    tiles_per_seq = S // tm
    sub = tm // ta
    const = lambda i: (0, 0)
    bf = jnp.bfloat16
    transposed = pl.BlockSpec((1, N_HEADS, sub, LANES, ta),
                              lambda i: (i // tiles_per_seq, 0, i % tiles_per_seq, 0, 0))
    return pl.pallas_call(
        functools.partial(_inproj_kernel, chunk=chunk),
        grid=(T // tm,),
        in_specs=[
            pl.BlockSpec((tm, D_MODEL), lambda i: (i, 0)),
            pl.BlockSpec((tm, 1), lambda i: (i, 0)),
            pl.BlockSpec((1, LANES), const),
            pl.BlockSpec((1, D_MODEL), const),
            pl.BlockSpec((D_MODEL, IN_WIDTH), const, pipeline_mode=pl.Buffered(1)),
        ],
        out_specs=[
            transposed,
            pl.BlockSpec((1, N_HEADS, tm, LANES), lambda i: (i // tiles_per_seq, 0, i % tiles_per_seq, 0)),
            transposed,
            pl.BlockSpec((tm, POOL_WIDTH), lambda i: (i, 0)),
            pl.BlockSpec((tm, 2 * D_MODEL), lambda i: (i, 0)),
        ],
        out_shape=[
            jax.ShapeDtypeStruct((B, N_HEADS, S // ta, LANES, ta), bf),
            jax.ShapeDtypeStruct((B, N_HEADS, S, LANES), bf),
            jax.ShapeDtypeStruct((B, N_HEADS, S // ta, LANES, ta), bf),
            jax.ShapeDtypeStruct((T, POOL_WIDTH), jnp.float32),
            jax.ShapeDtypeStruct((T, 2 * D_MODEL), bf),
        ],
        compiler_params=pltpu.CompilerParams(
            dimension_semantics=("arbitrary",), vmem_limit_bytes=VMEM_LIMIT),
        name="inproj",
    )(x2, pos2, invf, g, w_in)


def _attn_kernel(lq1_ref, lk1_ref, lq2_ref, lk2_ref, g_ref, qt_ref, k_ref, vt_ref, o_ref,
                 acc_ref, s_ref, *, lam_init):
    n_heads, n_tiles, _, ta = qt_ref.shape[1:]
    lam = (jnp.exp(jnp.sum(lq1_ref[...] * lk1_ref[...], axis=-1, keepdims=True))
           - jnp.exp(jnp.sum(lq2_ref[...] * lk2_ref[...], axis=-1, keepdims=True))
           + lam_init)
    first_map = lax.broadcasted_iota(jnp.int32, (LANES, STRIP), 0) < HEAD_DIM

    strips_per_map = ta // STRIP
    n_strips = 2 * strips_per_map
    items = [(h, c) for c in range(n_strips) for h in range(n_heads)]
    neg_inf = jnp.full((1, STRIP), -jnp.inf, jnp.float32)

    def scores(qi, j, h, c):
        q0 = (c % strips_per_map) * STRIP
        qt = qt_ref[0, h, qi, :, q0:q0 + STRIP]
        zero = jnp.zeros_like(qt)
        w = jnp.where(first_map, qt, zero) if c < strips_per_map else jnp.where(first_map, zero, qt)
        k0 = pl.multiple_of(j * ta, ta)
        s = jnp.dot(k_ref[0, h, pl.ds(k0, ta), :], w, preferred_element_type=jnp.float32)
        s_ref[h, c] = s
        return jnp.max(s, axis=0, keepdims=True)

    def accumulate(j, h, c, s_max, m_old, l_old, diagonal):
        if diagonal:
            q0 = (c % strips_per_map) * STRIP
            n_keys = q0 + STRIP
            causal = (lax.broadcasted_iota(jnp.int32, (STRIP, STRIP), 0)
                      <= lax.broadcasted_iota(jnp.int32, (STRIP, STRIP), 1))
            s = jnp.concatenate(
                [jnp.where(causal, s_ref[h, c, kb:kb + STRIP, :], -jnp.inf) if kb == q0
                 else s_ref[h, c, kb:kb + STRIP, :] for kb in range(0, n_keys, STRIP)], axis=0)
            s_max = jnp.max(s, axis=0, keepdims=True)
            vt = vt_ref[0, h, j, :, 0:n_keys]
        else:
            s = s_ref[h, c]
            vt = vt_ref[0, h, j]
        m_new = jnp.maximum(m_old, s_max)
        alpha = jnp.exp2(m_old - m_new)
        p = jnp.exp2(s - m_new)
        l_new = alpha * l_old + jnp.sum(p, axis=0, keepdims=True)
        pv = jnp.dot(vt, p.astype(jnp.bfloat16), preferred_element_type=jnp.float32)
        acc_ref[h, c] = alpha * acc_ref[h, c] + pv
        return m_new, l_new

    def finish(qi, h, l):
        q0 = pl.multiple_of(qi * ta, ta)
        for c in range(strips_per_map):
            c2 = strips_per_map + c
            o = acc_ref[h, c] * (1.0 / l[c]) - acc_ref[h, c2] * (lam / l[c2])
            scale = lax.rsqrt(jnp.mean(o * o, axis=0, keepdims=True) + RMS_EPS) * (1.0 - lam_init)
            o = o * scale * g_ref[...]
            o_ref[0, pl.ds(q0 + c * STRIP, STRIP), h * V_DIM:(h + 1) * V_DIM] = o.T.astype(o_ref.dtype)

    def step(qi, j, qi_next, j_next, state, diagonal):
        new, row_sums = [], {}
        for (h, c), (s_max, m, l) in zip(items, state):
            m, l = accumulate(j, h, c, s_max, m, l, diagonal)
            new.append((scores(qi_next, j_next, h, c), m, l))
            row_sums[h, c] = l
            if diagonal and all((h, cc) in row_sums for cc in range(n_strips)):
                finish(qi, h, [row_sums[h, cc] for cc in range(n_strips)])
        return tuple(new)

    def q_tile(qi, s_maxes):
        acc_ref[...] = jnp.zeros_like(acc_ref)
        init = tuple((s_max, neg_inf, jnp.zeros((1, STRIP), jnp.float32)) for s_max in s_maxes)

        def pair(i, st):
            st = step(qi, 2 * i, qi, 2 * i + 1, st, False)
            return step(qi, 2 * i + 1, qi, 2 * i + 2, st, False)

        state = lax.fori_loop(0, qi // 2, pair, init)
        state = lax.cond(qi % 2 == 1, lambda st: step(qi, qi - 1, qi, qi, st, False), lambda st: st, state)
        state = step(qi, qi, jnp.minimum(qi + 1, n_tiles - 1), 0, state, True)
        return tuple(st[0] for st in state)

    lax.fori_loop(0, n_tiles, q_tile, tuple(scores(0, 0, h, c) for h, c in items))


def _attention(qt, k, vt, lq1, lk1, lq2, lk2, subln_g, lam_init):
    B, H, n_tiles, _, ta = qt.shape
    S = n_tiles * ta
    hs = HEADS_PER_STEP
    vec = lambda n: pl.BlockSpec((1, n), lambda b, h: (0, 0))
    transposed = pl.BlockSpec((1, hs, n_tiles, LANES, ta), lambda b, h: (b, h, 0, 0, 0))
    return pl.pallas_call(
        functools.partial(_attn_kernel, lam_init=lam_init),
        grid=(B, H // hs),
        in_specs=[vec(HEAD_DIM), vec(HEAD_DIM), vec(HEAD_DIM), vec(HEAD_DIM),
                  pl.BlockSpec((V_DIM, 1), lambda b, h: (0, 0)),
                  transposed,
                  pl.BlockSpec((1, hs, S, LANES), lambda b, h: (b, h, 0, 0)),
                  transposed],
        out_specs=pl.BlockSpec((1, S, hs * V_DIM), lambda b, h: (b, 0, h)),
        out_shape=jax.ShapeDtypeStruct((B, S, H * V_DIM), jnp.bfloat16),
        scratch_shapes=[pltpu.VMEM((hs, 2 * ta // STRIP, V_DIM, STRIP), jnp.float32),
                        pltpu.VMEM((hs, 2 * ta // STRIP, ta, STRIP), jnp.float32)],
        compiler_params=pltpu.CompilerParams(
            dimension_semantics=("arbitrary", "arbitrary"), vmem_limit_bytes=VMEM_LIMIT),
        name="diff_attn",
    )(lq1, lk1, lq2, lk2, subln_g.reshape(V_DIM, 1), qt, k, vt)


def _mix_mlp_kernel(x_ref, a_ref, gate_ref, up_ref, uprev_ref, wpool_ref, pscale_ref, wout_ref,
                    g2_ref, wup_ref, wdown_ref, gf_ref, o_ref, ext_ref, *, tiles_per_seq, ff_chunk):
    tm = x_ref.shape[0]
    seq_tile = pl.program_id(0) % tiles_per_seq

    prev = uprev_ref[...]
    ext_ref[0:POOL_HALO, :] = jnp.where(seq_tile == 0, jnp.zeros_like(prev), prev)
    ext_ref[POOL_HALO:, :] = up_ref[...]

    def rows(r0, rn):
        rs = slice(r0, r0 + rn)
        s_idx = seq_tile * tm + r0 + lax.broadcasted_iota(jnp.int32, (rn, 1), 0)
        ys = []
        for g, w in enumerate(POOL_WINDOWS):
            cols = slice(g * POOL_GROUP_IN, (g + 1) * POOL_GROUP_IN)
            u = ext_ref[POOL_HALO + r0:POOL_HALO + r0 + rn, cols]
            win = u
            for j in range(1, w):
                win = win + ext_ref[POOL_HALO + r0 - j:POOL_HALO + r0 - j + rn, cols]
            cnt = jnp.minimum(s_idx + 1, w).astype(jnp.float32)
            d = win / cnt - u
            ys.append(jnp.dot(d.astype(jnp.bfloat16), wpool_ref[g], preferred_element_type=jnp.float32))
        p = jnp.concatenate(ys, axis=-1) * pscale_ref[...]

        merged = (gate_ref[rs, :D_MODEL].astype(jnp.float32) * a_ref[rs, :].astype(jnp.float32)
                  + gate_ref[rs, D_MODEL:].astype(jnp.float32) * p)
        x1 = x_ref[rs, :] + jnp.dot(merged.astype(jnp.bfloat16), wout_ref[...], preferred_element_type=jnp.float32)

        h2 = _rms(x1, g2_ref[...]).astype(jnp.bfloat16)
        acc = jnp.zeros((rn, D_MODEL), jnp.float32)
        for c0 in range(0, D_FF, ff_chunk):
            z = jnp.dot(h2, wup_ref[:, c0:c0 + ff_chunk], preferred_element_type=jnp.float32)
            z = jnp.square(jnp.maximum(z, 0.0)).astype(jnp.bfloat16)
            acc = acc + jnp.dot(z, wdown_ref[c0:c0 + ff_chunk, :], preferred_element_type=jnp.float32)
        o_ref[rs, :] = _rms(x1 + acc, gf_ref[...])

    rows(0, tm)


def _mix_mlp(x2, a2, gates, u_pool, w_pool, pool_scale, w_out, g2, w_up, w_down, gf, S, tm=TOKEN_TILE, ff_chunk=512):
    T = x2.shape[0]
    tiles_per_seq = S // tm
    halo_blocks = tm // POOL_HALO
    const2 = lambda i: (0, 0)
    resident = functools.partial(pl.BlockSpec, pipeline_mode=pl.Buffered(1))
    return pl.pallas_call(
        functools.partial(_mix_mlp_kernel, tiles_per_seq=tiles_per_seq, ff_chunk=ff_chunk),
        grid=(T // tm,),
        in_specs=[
            pl.BlockSpec((tm, D_MODEL), lambda i: (i, 0)),
            pl.BlockSpec((tm, D_MODEL), lambda i: (i, 0)),
            pl.BlockSpec((tm, 2 * D_MODEL), lambda i: (i, 0)),
            pl.BlockSpec((tm, POOL_WIDTH), lambda i: (i, 0)),
            pl.BlockSpec((POOL_HALO, POOL_WIDTH), lambda i: (jnp.maximum(i * halo_blocks - 1, 0), 0)),
            resident((len(POOL_WINDOWS), POOL_GROUP_IN, POOL_GROUP_OUT), lambda i: (0, 0, 0)),
            pl.BlockSpec((1, D_MODEL), const2),
            resident((D_MODEL, D_MODEL), const2),
            pl.BlockSpec((1, D_MODEL), const2),
            resident((D_MODEL, D_FF), const2),
            resident((D_FF, D_MODEL), const2),
            pl.BlockSpec((1, D_MODEL), const2),
        ],
        out_specs=pl.BlockSpec((tm, D_MODEL), lambda i: (i, 0)),
        out_shape=jax.ShapeDtypeStruct((T, D_MODEL), jnp.float32),
        scratch_shapes=[pltpu.VMEM((tm + POOL_HALO, POOL_WIDTH), jnp.float32)],
        compiler_params=pltpu.CompilerParams(
            dimension_semantics=("arbitrary",), vmem_limit_bytes=VMEM_LIMIT),
        name="mix_mlp",
    )(x2, a2, gates, u_pool, u_pool, w_pool, pool_scale, w_out, g2, w_up, w_down, gf)


def kernel(x, positions, norm_attn_g, w_in, lam_q1, lam_k1, lam_q2, lam_k2, subln_g, w_pool, pool_scale,
           w_out, norm_mlp_g, w_up, w_down, final_norm_g):
    B, S, D = x.shape
    assert D == D_MODEL and w_in.shape[0] == 1, "single-layer block of width 1024"
    lam_init = 0.8 - 0.6 * math.exp(-0.3 * 0)
    bf = jnp.bfloat16
    x2 = x.reshape(B * S, D)
    pos2 = positions.reshape(B * S, 1)
    inv_freq = ROPE_THETA ** (-jnp.arange(0, ROPE_DIM, 2, dtype=jnp.float32) / ROPE_DIM)
    invf = jnp.tile(inv_freq, LANES // (ROPE_DIM // 2)).reshape(1, LANES)

    qt, k, vt, u_pool, gates = _inproj(x2, pos2, invf, norm_attn_g, w_in[0].astype(bf), B, S)
    a = _attention(qt, k, vt, lam_q1, lam_k1, lam_q2, lam_k2, subln_g, lam_init)
    out = _mix_mlp(x2, a.reshape(B * S, D), gates, u_pool, w_pool[0].astype(bf), pool_scale,
                   w_out[0].astype(bf), norm_mlp_g, w_up[0].astype(bf), w_down[0].astype(bf),
                   final_norm_g.reshape(1, D), S)
    return out.reshape(B, S, D)
```

```python
import functools
import math

import jax
import jax.numpy as jnp
from jax import lax
from jax.experimental import pallas as pl
from jax.experimental.pallas import tpu as pltpu

D_MODEL = 1024
N_HEADS = 8
HEAD_DIM = 64
V_DIM = 128
ROPE_DIM = 16
ROPE_THETA = 500000.0
POOL_WINDOWS = (2, 4, 8, 16)
POOL_GROUP_IN = 128
POOL_GROUP_OUT = 256
POOL_WIDTH = 512
D_FF = 4096
RMS_EPS = 1e-6
LANES = 128
POOL_HALO = 16

QKV_COLS = 3 * D_MODEL
GATE_COL0 = QKV_COLS + POOL_WIDTH
IN_WIDTH = GATE_COL0 + 2 * D_MODEL

TOKEN_TILE = 512
INPROJ_TILE = 1024
ATTN_TILE = 512
STRIP = 256
HEADS_PER_STEP = 4
Q_SCALE = HEAD_DIM ** -0.5 * math.log2(math.e)
VMEM_LIMIT = 56 * 1024 * 1024


def _rms(x, g):
    return x * lax.rsqrt(jnp.mean(x * x, axis=-1, keepdims=True) + RMS_EPS) * g


def _inproj_kernel(x_ref, pos_ref, invf_ref, g_ref, w_ref, qt_ref, k_ref, vt_ref, pool_ref, gate_ref, *,
                   chunk, tiles_per_seq):
    tm = x_ref.shape[0]
    ta = qt_ref.shape[-1]
    half = ROPE_DIM // 2
    h = _rms(x_ref[...], g_ref[...]).astype(jnp.bfloat16)

    step = pl.program_id(0)
    pos = pos_ref[pl.ds(step // tiles_per_seq, 1), :].astype(jnp.float32)
    ang = invf_ref[...] * pos
    cos, sin = jnp.cos(ang), jnp.sin(ang)

    ones = jnp.ones((HEAD_DIM - ROPE_DIM, tm), jnp.float32)
    zeros = jnp.zeros((half, tm), jnp.float32)
    rest0 = jnp.zeros((HEAD_DIM - ROPE_DIM, tm), jnp.float32)
    c_tab = jnp.concatenate([cos, cos, ones] * 2, axis=0).T
    s_lo = jnp.concatenate([-sin, zeros, rest0] * 2, axis=0).T
    s_hi = jnp.concatenate([zeros, sin, rest0] * 2, axis=0).T

    def rope(t):
        return (t * c_tab + pltpu.roll(t, LANES - half, 1) * s_lo + pltpu.roll(t, half, 1) * s_hi)

    def store_transposed(ref, head, t, rotary_scale=None):
        for c in range(tm // ta):
            tt = t[c * ta:(c + 1) * ta].T
            if rotary_scale is not None:
                cc, ss = cos[:, c * ta:(c + 1) * ta], sin[:, c * ta:(c + 1) * ta]
                blocks = [tt[r:r + half] for r in range(0, LANES, half)]
                for b0 in (0, HEAD_DIM // half):
                    lo, hi = blocks[b0], blocks[b0 + 1]
                    blocks[b0], blocks[b0 + 1] = lo * cc - hi * ss, hi * cc + lo * ss
                tt = jnp.concatenate(blocks, axis=0) * rotary_scale
            ref[0, head, c] = tt.astype(ref.dtype)

    starts = list(range(0, IN_WIDTH, chunk))
    order = ([c0 for c0 in starts if c0 >= GATE_COL0] + [c0 for c0 in starts if 2 * D_MODEL <= c0 < QKV_COLS]
             + [c0 for c0 in starts if c0 < 2 * D_MODEL] + [c0 for c0 in starts if QKV_COLS <= c0 < GATE_COL0])
    for c0 in order:
        r = jnp.dot(h, w_ref[:, c0:c0 + chunk], preferred_element_type=jnp.float32)
        if c0 < QKV_COLS:
            for j in range(chunk // LANES):
                slab = (c0 // LANES) + j
                t = r[:, j * LANES:(j + 1) * LANES]
                if slab < N_HEADS:
                    store_transposed(qt_ref, slab, t, rotary_scale=Q_SCALE)
                elif slab < 2 * N_HEADS:
                    k_ref[0, slab - N_HEADS] = rope(t).astype(k_ref.dtype)
                else:
                    store_transposed(vt_ref, slab - 2 * N_HEADS, t)
        elif c0 < GATE_COL0:
            pool_ref[:, c0 - QKV_COLS:c0 - QKV_COLS + chunk] = r
        else:
            gate_ref[:, c0 - GATE_COL0:c0 - GATE_COL0 + chunk] = jax.nn.sigmoid(r).astype(gate_ref.dtype)


def _inproj(x2, pos2, invf, g, w_in, B, S, tm=INPROJ_TILE, ta=ATTN_TILE, chunk=512):
    T = x2.shape[0]
    tiles_per_seq = S // tm
    sub = tm // ta
    const = lambda i: (0, 0)
    bf = jnp.bfloat16
    transposed = pl.BlockSpec((1, N_HEADS, sub, LANES, ta),
                              lambda i: (i // tiles_per_seq, 0, i % tiles_per_seq, 0, 0))
    return pl.pallas_call(
        functools.partial(_inproj_kernel, chunk=chunk, tiles_per_seq=tiles_per_seq),
        grid=(T // tm,),
        in_specs=[
            pl.BlockSpec((tm, D_MODEL), lambda i: (i, 0)),
            pl.BlockSpec((B, tm), lambda i: (0, i % tiles_per_seq)),
            pl.BlockSpec((ROPE_DIM // 2, 1), const),
            pl.BlockSpec((1, D_MODEL), const),
            pl.BlockSpec((D_MODEL, IN_WIDTH), const, pipeline_mode=pl.Buffered(1)),
        ],
        out_specs=[
            transposed,
            pl.BlockSpec((1, N_HEADS, tm, LANES), lambda i: (i // tiles_per_seq, 0, i % tiles_per_seq, 0)),
            transposed,
            pl.BlockSpec((tm, POOL_WIDTH), lambda i: (i, 0)),
            pl.BlockSpec((tm, 2 * D_MODEL), lambda i: (i, 0)),
        ],
        out_shape=[
            jax.ShapeDtypeStruct((B, N_HEADS, S // ta, LANES, ta), bf),
            jax.ShapeDtypeStruct((B, N_HEADS, S, LANES), bf),
            jax.ShapeDtypeStruct((B, N_HEADS, S // ta, LANES, ta), bf),
            jax.ShapeDtypeStruct((T, POOL_WIDTH), jnp.float32),
            jax.ShapeDtypeStruct((T, 2 * D_MODEL), bf),
        ],
        compiler_params=pltpu.CompilerParams(
            dimension_semantics=("arbitrary",), vmem_limit_bytes=VMEM_LIMIT),
        name="inproj",
    )(x2, pos2, invf, g, w_in)


def _attn_kernel(lq1_ref, lk1_ref, lq2_ref, lk2_ref, g_ref, qt_ref, k_ref, vt_ref, o_ref,
                 acc_ref, s_ref, *, lam_init):
    n_heads, n_tiles, _, ta = qt_ref.shape[1:]
    lam = (jnp.exp(jnp.sum(lq1_ref[...] * lk1_ref[...], axis=-1, keepdims=True))
           - jnp.exp(jnp.sum(lq2_ref[...] * lk2_ref[...], axis=-1, keepdims=True))
           + lam_init)
    first_map = lax.broadcasted_iota(jnp.int32, (LANES, STRIP), 0) < HEAD_DIM

    strips_per_map = ta // STRIP
    n_strips = 2 * strips_per_map
    items = [(h, c) for c in range(n_strips) for h in range(n_heads)]
    neg_inf = jnp.full((1, STRIP), -jnp.inf, jnp.float32)

    def scores(qi, j, h, c):
        q0 = (c % strips_per_map) * STRIP
        qt = qt_ref[0, h, qi, :, q0:q0 + STRIP]
        zero = jnp.zeros_like(qt)
        w = jnp.where(first_map, qt, zero) if c < strips_per_map else jnp.where(first_map, zero, qt)
        k0 = pl.multiple_of(j * ta, ta)
        s = jnp.dot(k_ref[0, h, pl.ds(k0, ta), :], w, preferred_element_type=jnp.float32)
        s_ref[h, c] = s
        return jnp.max(s, axis=0, keepdims=True)

    def accumulate(j, h, c, s_max, m_old, l_old, diagonal):
        if diagonal:
            q0 = (c % strips_per_map) * STRIP
            n_keys = q0 + STRIP
            causal = (lax.broadcasted_iota(jnp.int32, (STRIP, STRIP), 0)
                      <= lax.broadcasted_iota(jnp.int32, (STRIP, STRIP), 1))
            s = jnp.concatenate(
                [jnp.where(causal, s_ref[h, c, kb:kb + STRIP, :], -jnp.inf) if kb == q0
                 else s_ref[h, c, kb:kb + STRIP, :] for kb in range(0, n_keys, STRIP)], axis=0)
            s_max = jnp.max(s, axis=0, keepdims=True)
            vt = vt_ref[0, h, j, :, 0:n_keys]
        else:
            s = s_ref[h, c]
            vt = vt_ref[0, h, j]
        m_new = jnp.maximum(m_old, s_max)
        alpha = jnp.exp2(m_old - m_new)
        p = jnp.exp2(s - m_new)
        l_new = alpha * l_old + jnp.sum(p, axis=0, keepdims=True)
        pv = jnp.dot(vt, p.astype(jnp.bfloat16), preferred_element_type=jnp.float32)
        acc_ref[h, c] = alpha * acc_ref[h, c] + pv
        return m_new, l_new

    def finish(qi, h, l):
        q0 = pl.multiple_of(qi * ta, ta)
        for c in range(strips_per_map):
            c2 = strips_per_map + c
            o = acc_ref[h, c] * (1.0 / l[c]) - acc_ref[h, c2] * (lam / l[c2])
            scale = lax.rsqrt(jnp.mean(o * o, axis=0, keepdims=True) + RMS_EPS) * (1.0 - lam_init)
            o = o * scale * g_ref[...]
            o_ref[0, pl.ds(q0 + c * STRIP, STRIP), h * V_DIM:(h + 1) * V_DIM] = o.T.astype(o_ref.dtype)

    def step(qi, j, qi_next, j_next, state, diagonal):
        new, row_sums = [], {}
        for (h, c), (s_max, m, l) in zip(items, state):
            m, l = accumulate(j, h, c, s_max, m, l, diagonal)
            new.append((scores(qi_next, j_next, h, c), m, l))
            row_sums[h, c] = l
            if diagonal and all((h, cc) in row_sums for cc in range(n_strips)):
                finish(qi, h, [row_sums[h, cc] for cc in range(n_strips)])
        return tuple(new)

    def q_tile(qi, s_maxes):
        acc_ref[...] = jnp.zeros_like(acc_ref)
        init = tuple((s_max, neg_inf, jnp.zeros((1, STRIP), jnp.float32)) for s_max in s_maxes)

        def pair(i, st):
            st = step(qi, 2 * i, qi, 2 * i + 1, st, False)
            return step(qi, 2 * i + 1, qi, 2 * i + 2, st, False)

        state = lax.fori_loop(0, qi // 2, pair, init)
        state = lax.cond(qi % 2 == 1, lambda st: step(qi, qi - 1, qi, qi, st, False), lambda st: st, state)
        state = step(qi, qi, jnp.minimum(qi + 1, n_tiles - 1), 0, state, True)
        return tuple(st[0] for st in state)

    lax.fori_loop(0, n_tiles, q_tile, tuple(scores(0, 0, h, c) for h, c in items))


def _attention(qt, k, vt, lq1, lk1, lq2, lk2, subln_g, lam_init):
    B, H, n_tiles, _, ta = qt.shape
    S = n_tiles * ta
    hs = HEADS_PER_STEP
    vec = lambda n: pl.BlockSpec((1, n), lambda b, h: (0, 0))
    transposed = pl.BlockSpec((1, hs, n_tiles, LANES, ta), lambda b, h: (b, h, 0, 0, 0))
    return pl.pallas_call(
        functools.partial(_attn_kernel, lam_init=lam_init),
        grid=(B, H // hs),
        in_specs=[vec(HEAD_DIM), vec(HEAD_DIM), vec(HEAD_DIM), vec(HEAD_DIM),
                  pl.BlockSpec((V_DIM, 1), lambda b, h: (0, 0)),
                  transposed,
                  pl.BlockSpec((1, hs, S, LANES), lambda b, h: (b, h, 0, 0)),
                  transposed],
        out_specs=pl.BlockSpec((1, S, hs * V_DIM), lambda b, h: (b, 0, h)),
        out_shape=jax.ShapeDtypeStruct((B, S, H * V_DIM), jnp.bfloat16),
        scratch_shapes=[pltpu.VMEM((hs, 2 * ta // STRIP, V_DIM, STRIP), jnp.float32),
                        pltpu.VMEM((hs, 2 * ta // STRIP, ta, STRIP), jnp.float32)],
        compiler_params=pltpu.CompilerParams(
            dimension_semantics=("arbitrary", "arbitrary"), vmem_limit_bytes=VMEM_LIMIT),
        name="diff_attn",
    )(lq1, lk1, lq2, lk2, subln_g.reshape(V_DIM, 1), qt, k, vt)


def _mix_mlp_kernel(x_ref, a_ref, gate_ref, up_ref, uprev_ref, wpool_ref, pscale_ref, wout_ref,
                    g2_ref, wup_ref, wdown_ref, gf_ref, o_ref, ext_ref, *, tiles_per_seq, ff_chunk):
    tm = x_ref.shape[0]
    seq_tile = pl.program_id(0) % tiles_per_seq
    bf = jnp.bfloat16

    prev = uprev_ref[...]
    ext_ref[0:POOL_HALO, :] = jnp.where(seq_tile == 0, jnp.zeros_like(prev), prev)
    ext_ref[POOL_HALO:, :] = up_ref[...]
    s_idx = seq_tile * tm + lax.broadcasted_iota(jnp.int32, (tm, 1), 0)
    ys = []
    for g, w in enumerate(POOL_WINDOWS):
        cols = slice(g * POOL_GROUP_IN, (g + 1) * POOL_GROUP_IN)
        u = ext_ref[POOL_HALO:, cols]
        win = u
        for j in range(1, w):
            win = win + ext_ref[POOL_HALO - j:POOL_HALO - j + tm, cols]
        cnt = jnp.minimum(s_idx + 1, w).astype(jnp.float32)
        d = win / cnt - u
        ys.append(jnp.dot(d.astype(bf), wpool_ref[g], preferred_element_type=jnp.float32))
    p = jnp.concatenate(ys, axis=-1) * pscale_ref[...]

    merged = (gate_ref[:, :D_MODEL].astype(jnp.float32) * a_ref[...].astype(jnp.float32)
              + gate_ref[:, D_MODEL:].astype(jnp.float32) * p)
    x1 = x_ref[...] + jnp.dot(merged.astype(bf), wout_ref[...], preferred_element_type=jnp.float32)

    h2 = _rms(x1, g2_ref[...]).astype(bf)
    acc = jnp.zeros((tm, D_MODEL), jnp.float32)
    for c0 in range(0, D_FF, ff_chunk):
        z = jnp.dot(h2, wup_ref[:, c0:c0 + ff_chunk], preferred_element_type=jnp.float32)
        z = jnp.square(jnp.maximum(z, 0.0)).astype(bf)
        acc = acc + jnp.dot(z, wdown_ref[c0:c0 + ff_chunk, :], preferred_element_type=jnp.float32)
    o_ref[...] = _rms(x1 + acc, gf_ref[...])


def _mix_mlp(x2, a2, gates, u_pool, w_pool, pool_scale, w_out, g2, w_up, w_down, gf, S, tm=TOKEN_TILE, ff_chunk=512):
    T = x2.shape[0]
    tiles_per_seq = S // tm
    halo_blocks = tm // POOL_HALO
    const2 = lambda i: (0, 0)
    resident = functools.partial(pl.BlockSpec, pipeline_mode=pl.Buffered(1))
    return pl.pallas_call(
        functools.partial(_mix_mlp_kernel, tiles_per_seq=tiles_per_seq, ff_chunk=ff_chunk),
        grid=(T // tm,),
        in_specs=[
            pl.BlockSpec((tm, D_MODEL), lambda i: (i, 0)),
            pl.BlockSpec((tm, D_MODEL), lambda i: (i, 0)),
            pl.BlockSpec((tm, 2 * D_MODEL), lambda i: (i, 0)),
            pl.BlockSpec((tm, POOL_WIDTH), lambda i: (i, 0)),
            pl.BlockSpec((POOL_HALO, POOL_WIDTH), lambda i: (jnp.maximum(i * halo_blocks - 1, 0), 0)),
            resident((len(POOL_WINDOWS), POOL_GROUP_IN, POOL_GROUP_OUT), lambda i: (0, 0, 0)),
            pl.BlockSpec((1, D_MODEL), const2),
            resident((D_MODEL, D_MODEL), const2),
            pl.BlockSpec((1, D_MODEL), const2),
            resident((D_MODEL, D_FF), const2),
            resident((D_FF, D_MODEL), const2),
            pl.BlockSpec((1, D_MODEL), const2),
        ],
        out_specs=pl.BlockSpec((tm, D_MODEL), lambda i: (i, 0)),
        out_shape=jax.ShapeDtypeStruct((T, D_MODEL), jnp.float32),
        scratch_shapes=[pltpu.VMEM((tm + POOL_HALO, POOL_WIDTH), jnp.float32)],
        compiler_params=pltpu.CompilerParams(
            dimension_semantics=("arbitrary",), vmem_limit_bytes=VMEM_LIMIT),
        name="mix_mlp",
    )(x2, a2, gates, u_pool, u_pool, w_pool, pool_scale, w_out, g2, w_up, w_down, gf)


def kernel(x, positions, norm_attn_g, w_in, lam_q1, lam_k1, lam_q2, lam_k2, subln_g, w_pool, pool_scale,
           w_out, norm_mlp_g, w_up, w_down, final_norm_g):
    B, S, D = x.shape
    assert D == D_MODEL and w_in.shape[0] == 1, "single-layer block of width 1024"
    lam_init = 0.8 - 0.6 * math.exp(-0.3 * 0)
    bf = jnp.bfloat16
    x2 = x.reshape(B * S, D)
    inv_freq = ROPE_THETA ** (-jnp.arange(0, ROPE_DIM, 2, dtype=jnp.float32) / ROPE_DIM)
    invf = inv_freq.reshape(ROPE_DIM // 2, 1)

    qt, k, vt, u_pool, gates = _inproj(x2, positions, invf, norm_attn_g, w_in[0].astype(bf), B, S)
    a = _attention(qt, k, vt, lam_q1, lam_k1, lam_q2, lam_k2, subln_g, lam_init)
    out = _mix_mlp(x2, a.reshape(B * S, D), gates, u_pool, w_pool[0].astype(bf), pool_scale,
                   w_out[0].astype(bf), norm_mlp_g, w_up[0].astype(bf), w_down[0].astype(bf),
                   final_norm_g.reshape(1, D), S)
    return out.reshape(B, S, D)
```

```python
import functools
import math

import jax
import jax.numpy as jnp
from jax import lax
from jax.experimental import pallas as pl
from jax.experimental.pallas import tpu as pltpu

D_MODEL = 1024
N_HEADS = 8
HEAD_DIM = 64
V_DIM = 128
ROPE_DIM = 16
ROPE_THETA = 500000.0
POOL_WINDOWS = (2, 4, 8, 16)
POOL_GROUP_IN = 128
POOL_GROUP_OUT = 256
POOL_WIDTH = 512
D_FF = 4096
RMS_EPS = 1e-6
LANES = 128
POOL_HALO = 16

QKV_COLS = 3 * D_MODEL
GATE_COL0 = QKV_COLS + POOL_WIDTH
IN_WIDTH = GATE_COL0 + 2 * D_MODEL

TOKEN_TILE = 512
INPROJ_TILE = 1024
ATTN_TILE = 512
STRIP = 256
HEADS_PER_STEP = 4
SUM_ROWS = 16
Q_SCALE = HEAD_DIM ** -0.5 * math.log2(math.e)
VMEM_LIMIT = 56 * 1024 * 1024


def _rms(x, g):
    return x * lax.rsqrt(jnp.mean(x * x, axis=-1, keepdims=True) + RMS_EPS) * g


def _inproj_kernel(x_ref, pos_ref, invf_ref, g_ref, w_ref, qt_ref, k_ref, vt_ref, pool_ref, gate_ref, *,
                   chunk, tiles_per_seq):
    tm = x_ref.shape[0]
    ta = qt_ref.shape[-1]
    half = ROPE_DIM // 2
    h = _rms(x_ref[...], g_ref[...]).astype(jnp.bfloat16)

    step = pl.program_id(0)
    pos = pos_ref[pl.ds(step // tiles_per_seq, 1), :].astype(jnp.float32)
    ang = invf_ref[...] * pos
    cos, sin = jnp.cos(ang), jnp.sin(ang)

    ones = jnp.ones((HEAD_DIM - ROPE_DIM, tm), jnp.float32)
    zeros = jnp.zeros((half, tm), jnp.float32)
    rest0 = jnp.zeros((HEAD_DIM - ROPE_DIM, tm), jnp.float32)
    c_tab = jnp.concatenate([cos, cos, ones] * 2, axis=0).T
    s_lo = jnp.concatenate([-sin, zeros, rest0] * 2, axis=0).T
    s_hi = jnp.concatenate([zeros, sin, rest0] * 2, axis=0).T

    def rope(t):
        return (t * c_tab + pltpu.roll(t, LANES - half, 1) * s_lo + pltpu.roll(t, half, 1) * s_hi)

    def store_transposed(ref, head, t, rotary_scale=None):
        for c in range(tm // ta):
            tt = t[c * ta:(c + 1) * ta].T
            if rotary_scale is not None:
                cc, ss = cos[:, c * ta:(c + 1) * ta], sin[:, c * ta:(c + 1) * ta]
                blocks = [tt[r:r + half] for r in range(0, LANES, half)]
                for b0 in (0, HEAD_DIM // half):
                    lo, hi = blocks[b0], blocks[b0 + 1]
                    blocks[b0], blocks[b0 + 1] = lo * cc - hi * ss, hi * cc + lo * ss
                tt = jnp.concatenate(blocks, axis=0) * rotary_scale
            ref[0, head, c] = tt.astype(ref.dtype)

    starts = list(range(0, IN_WIDTH, chunk))
    order = ([c0 for c0 in starts if c0 >= GATE_COL0] + [c0 for c0 in starts if 2 * D_MODEL <= c0 < QKV_COLS]
             + [c0 for c0 in starts if c0 < 2 * D_MODEL] + [c0 for c0 in starts if QKV_COLS <= c0 < GATE_COL0])
    for c0 in order:
        r = jnp.dot(h, w_ref[:, c0:c0 + chunk], preferred_element_type=jnp.float32)
        if c0 < QKV_COLS:
            for j in range(chunk // LANES):
                slab = (c0 // LANES) + j
                t = r[:, j * LANES:(j + 1) * LANES]
                if slab < N_HEADS:
                    store_transposed(qt_ref, slab, t, rotary_scale=Q_SCALE)
                elif slab < 2 * N_HEADS:
                    k_ref[0, slab - N_HEADS] = rope(t).astype(k_ref.dtype)
                else:
                    store_transposed(vt_ref, slab - 2 * N_HEADS, t)
        elif c0 < GATE_COL0:
            pool_ref[:, c0 - QKV_COLS:c0 - QKV_COLS + chunk] = r
        else:
            gate_ref[:, c0 - GATE_COL0:c0 - GATE_COL0 + chunk] = jax.nn.sigmoid(r).astype(gate_ref.dtype)


def _inproj(x2, pos2, invf, g, w_in, B, S, tm=INPROJ_TILE, ta=ATTN_TILE, chunk=512):
    T = x2.shape[0]
    tiles_per_seq = S // tm
    sub = tm // ta
    const = lambda i: (0, 0)
    bf = jnp.bfloat16
    transposed = pl.BlockSpec((1, N_HEADS, sub, LANES, ta),
                              lambda i: (i // tiles_per_seq, 0, i % tiles_per_seq, 0, 0))
    return pl.pallas_call(
        functools.partial(_inproj_kernel, chunk=chunk, tiles_per_seq=tiles_per_seq),
        grid=(T // tm,),
        in_specs=[
            pl.BlockSpec((tm, D_MODEL), lambda i: (i, 0)),
            pl.BlockSpec((B, tm), lambda i: (0, i % tiles_per_seq)),
            pl.BlockSpec((ROPE_DIM // 2, 1), const),
            pl.BlockSpec((1, D_MODEL), const),
            pl.BlockSpec((D_MODEL, IN_WIDTH), const, pipeline_mode=pl.Buffered(1)),
        ],
        out_specs=[
            transposed,
            pl.BlockSpec((1, N_HEADS, tm, LANES), lambda i: (i // tiles_per_seq, 0, i % tiles_per_seq, 0)),
            transposed,
            pl.BlockSpec((tm, POOL_WIDTH), lambda i: (i, 0)),
            pl.BlockSpec((tm, 2 * D_MODEL), lambda i: (i, 0)),
        ],
        out_shape=[
            jax.ShapeDtypeStruct((B, N_HEADS, S // ta, LANES, ta), bf),
            jax.ShapeDtypeStruct((B, N_HEADS, S, LANES), bf),
            jax.ShapeDtypeStruct((B, N_HEADS, S // ta, LANES, ta), bf),
            jax.ShapeDtypeStruct((T, POOL_WIDTH), jnp.float32),
            jax.ShapeDtypeStruct((T, 2 * D_MODEL), bf),
        ],
        compiler_params=pltpu.CompilerParams(
            dimension_semantics=("arbitrary",), vmem_limit_bytes=VMEM_LIMIT),
        name="inproj",
    )(x2, pos2, invf, g, w_in)


def _attn_kernel(lq1_ref, lk1_ref, lq2_ref, lk2_ref, g_ref, qt_ref, k_ref, vt_ref, o_ref,
                 acc_ref, s_ref, *, lam_init):
    n_heads, n_tiles, _, ta = qt_ref.shape[1:]
    lam = (jnp.exp(jnp.sum(lq1_ref[...] * lk1_ref[...], axis=-1, keepdims=True))
           - jnp.exp(jnp.sum(lq2_ref[...] * lk2_ref[...], axis=-1, keepdims=True))
           + lam_init)
    first_map = lax.broadcasted_iota(jnp.int32, (LANES, STRIP), 0) < HEAD_DIM

    strips_per_map = ta // STRIP
    n_strips = 2 * strips_per_map
    items = [(h, c) for c in range(n_strips) for h in range(n_heads)]
    neg_inf = jnp.full((1, STRIP), -jnp.inf, jnp.float32)

    def scores(qi, j, h, c):
        q0 = (c % strips_per_map) * STRIP
        qt = qt_ref[0, h, qi, :, q0:q0 + STRIP]
        zero = jnp.zeros_like(qt)
        w = jnp.where(first_map, qt, zero) if c < strips_per_map else jnp.where(first_map, zero, qt)
        k0 = pl.multiple_of(j * ta, ta)
        s = jnp.dot(k_ref[0, h, pl.ds(k0, ta), :], w, preferred_element_type=jnp.float32)
        s_ref[h, c] = s
        return jnp.max(s, axis=0, keepdims=True)

    def accumulate(j, h, c, s_max, m_old, l_old, diagonal):
        if diagonal:
            q0 = (c % strips_per_map) * STRIP
            n_keys = q0 + STRIP
            causal = (lax.broadcasted_iota(jnp.int32, (STRIP, STRIP), 0)
                      <= lax.broadcasted_iota(jnp.int32, (STRIP, STRIP), 1))
            s = jnp.concatenate(
                [jnp.where(causal, s_ref[h, c, kb:kb + STRIP, :], -jnp.inf) if kb == q0
                 else s_ref[h, c, kb:kb + STRIP, :] for kb in range(0, n_keys, STRIP)], axis=0)
            s_max = jnp.max(s, axis=0, keepdims=True)
            vt = vt_ref[0, h, j, :, 0:n_keys]
        else:
            s = s_ref[h, c]
            vt = vt_ref[0, h, j]
        m_new = jnp.maximum(m_old, s_max)
        alpha = jnp.exp2(m_old - m_new)
        p = jnp.exp2((s - m_new).astype(jnp.bfloat16))
        ones = jnp.ones((SUM_ROWS, vt.shape[1]), vt.dtype)
        pv = jnp.dot(jnp.concatenate([vt, ones], axis=0), p, preferred_element_type=jnp.float32)
        l_new = alpha * l_old + pv[V_DIM:V_DIM + 1]
        acc_ref[h, c] = alpha * acc_ref[h, c] + pv[:V_DIM]
        return m_new, l_new

    def finish(qi, h, l):
        q0 = pl.multiple_of(qi * ta, ta)
        for c in range(strips_per_map):
            c2 = strips_per_map + c
            o = acc_ref[h, c] * (1.0 / l[c]) - acc_ref[h, c2] * (lam / l[c2])
            scale = lax.rsqrt(jnp.mean(o * o, axis=0, keepdims=True) + RMS_EPS) * (1.0 - lam_init)
            o = o * scale * g_ref[...]
            o_ref[0, pl.ds(q0 + c * STRIP, STRIP), h * V_DIM:(h + 1) * V_DIM] = o.T.astype(o_ref.dtype)

    def step(qi, j, qi_next, j_next, state, diagonal):
        new, row_sums = [], {}
        for (h, c), (s_max, m, l) in zip(items, state):
            m, l = accumulate(j, h, c, s_max, m, l, diagonal)
            new.append((scores(qi_next, j_next, h, c), m, l))
            row_sums[h, c] = l
            if diagonal and all((h, cc) in row_sums for cc in range(n_strips)):
                finish(qi, h, [row_sums[h, cc] for cc in range(n_strips)])
        return tuple(new)

    def q_tile(qi, s_maxes):
        acc_ref[...] = jnp.zeros_like(acc_ref)
        init = tuple((s_max, neg_inf, jnp.zeros((1, STRIP), jnp.float32)) for s_max in s_maxes)

        def pair(i, st):
            st = step(qi, 2 * i, qi, 2 * i + 1, st, False)
            return step(qi, 2 * i + 1, qi, 2 * i + 2, st, False)

        state = lax.fori_loop(0, qi // 2, pair, init)
        state = lax.cond(qi % 2 == 1, lambda st: step(qi, qi - 1, qi, qi, st, False), lambda st: st, state)
        state = step(qi, qi, jnp.minimum(qi + 1, n_tiles - 1), 0, state, True)
        return tuple(st[0] for st in state)

    lax.fori_loop(0, n_tiles, q_tile, tuple(scores(0, 0, h, c) for h, c in items))


def _attention(qt, k, vt, lq1, lk1, lq2, lk2, subln_g, lam_init):
    B, H, n_tiles, _, ta = qt.shape
    S = n_tiles * ta
    hs = HEADS_PER_STEP
    vec = lambda n: pl.BlockSpec((1, n), lambda b, h: (0, 0))
    transposed = pl.BlockSpec((1, hs, n_tiles, LANES, ta), lambda b, h: (b, h, 0, 0, 0))
    return pl.pallas_call(
        functools.partial(_attn_kernel, lam_init=lam_init),
        grid=(B, H // hs),
        in_specs=[vec(HEAD_DIM), vec(HEAD_DIM), vec(HEAD_DIM), vec(HEAD_DIM),
                  pl.BlockSpec((V_DIM, 1), lambda b, h: (0, 0)),
                  transposed,
                  pl.BlockSpec((1, hs, S, LANES), lambda b, h: (b, h, 0, 0)),
                  transposed],
        out_specs=pl.BlockSpec((1, S, hs * V_DIM), lambda b, h: (b, 0, h)),
        out_shape=jax.ShapeDtypeStruct((B, S, H * V_DIM), jnp.bfloat16),
        scratch_shapes=[pltpu.VMEM((hs, 2 * ta // STRIP, V_DIM, STRIP), jnp.float32),
                        pltpu.VMEM((hs, 2 * ta // STRIP, ta, STRIP), jnp.float32)],
        compiler_params=pltpu.CompilerParams(
            dimension_semantics=("arbitrary", "arbitrary"), vmem_limit_bytes=VMEM_LIMIT),
        name="diff_attn",
    )(lq1, lk1, lq2, lk2, subln_g.reshape(V_DIM, 1), qt, k, vt)


def _mix_mlp_kernel(x_ref, a_ref, gate_ref, up_ref, uprev_ref, wpool_ref, pscale_ref, wout_ref,
                    g2_ref, wup_ref, wdown_ref, gf_ref, o_ref, ext_ref, *, tiles_per_seq, ff_chunk):
    tm = x_ref.shape[0]
    seq_tile = pl.program_id(0) % tiles_per_seq
    bf = jnp.bfloat16

    prev = uprev_ref[...]
    ext_ref[0:POOL_HALO, :] = jnp.where(seq_tile == 0, jnp.zeros_like(prev), prev)
    ext_ref[POOL_HALO:, :] = up_ref[...]
    s_idx = seq_tile * tm + lax.broadcasted_iota(jnp.int32, (tm, 1), 0)
    ys = []
    for g, w in enumerate(POOL_WINDOWS):
        cols = slice(g * POOL_GROUP_IN, (g + 1) * POOL_GROUP_IN)
        u = ext_ref[POOL_HALO:, cols]
        win = u
        for j in range(1, w):
            win = win + ext_ref[POOL_HALO - j:POOL_HALO - j + tm, cols]
        cnt = jnp.minimum(s_idx + 1, w).astype(jnp.float32)
        d = win / cnt - u
        ys.append(jnp.dot(d.astype(bf), wpool_ref[g], preferred_element_type=jnp.float32))
    p = jnp.concatenate(ys, axis=-1) * pscale_ref[...]

    merged = (gate_ref[:, :D_MODEL].astype(jnp.float32) * a_ref[...].astype(jnp.float32)
              + gate_ref[:, D_MODEL:].astype(jnp.float32) * p)
    x1 = x_ref[...] + jnp.dot(merged.astype(bf), wout_ref[...], preferred_element_type=jnp.float32)

    h2 = _rms(x1, g2_ref[...]).astype(bf)
    acc = jnp.zeros((tm, D_MODEL), jnp.float32)
    for c0 in range(0, D_FF, ff_chunk):
        z = jnp.dot(h2, wup_ref[:, c0:c0 + ff_chunk], preferred_element_type=jnp.float32)
        z = jnp.square(jnp.maximum(z, 0.0)).astype(bf)
        acc = acc + jnp.dot(z, wdown_ref[c0:c0 + ff_chunk, :], preferred_element_type=jnp.float32)
    o_ref[...] = _rms(x1 + acc, gf_ref[...])


def _mix_mlp(x2, a2, gates, u_pool, w_pool, pool_scale, w_out, g2, w_up, w_down, gf, S, tm=TOKEN_TILE, ff_chunk=512):
    T = x2.shape[0]
    tiles_per_seq = S // tm
    halo_blocks = tm // POOL_HALO
    const2 = lambda i: (0, 0)
    resident = functools.partial(pl.BlockSpec, pipeline_mode=pl.Buffered(1))
    return pl.pallas_call(
        functools.partial(_mix_mlp_kernel, tiles_per_seq=tiles_per_seq, ff_chunk=ff_chunk),
        grid=(T // tm,),
        in_specs=[
            pl.BlockSpec((tm, D_MODEL), lambda i: (i, 0)),
            pl.BlockSpec((tm, D_MODEL), lambda i: (i, 0)),
            pl.BlockSpec((tm, 2 * D_MODEL), lambda i: (i, 0)),
            pl.BlockSpec((tm, POOL_WIDTH), lambda i: (i, 0)),
            pl.BlockSpec((POOL_HALO, POOL_WIDTH), lambda i: (jnp.maximum(i * halo_blocks - 1, 0), 0)),
            resident((len(POOL_WINDOWS), POOL_GROUP_IN, POOL_GROUP_OUT), lambda i: (0, 0, 0)),
            pl.BlockSpec((1, D_MODEL), const2),
            resident((D_MODEL, D_MODEL), const2),
            pl.BlockSpec((1, D_MODEL), const2),
            resident((D_MODEL, D_FF), const2),
            resident((D_FF, D_MODEL), const2),
            pl.BlockSpec((1, D_MODEL), const2),
        ],
        out_specs=pl.BlockSpec((tm, D_MODEL), lambda i: (i, 0)),
        out_shape=jax.ShapeDtypeStruct((T, D_MODEL), jnp.float32),
        scratch_shapes=[pltpu.VMEM((tm + POOL_HALO, POOL_WIDTH), jnp.float32)],
        compiler_params=pltpu.CompilerParams(
            dimension_semantics=("arbitrary",), vmem_limit_bytes=VMEM_LIMIT),
        name="mix_mlp",
    )(x2, a2, gates, u_pool, u_pool, w_pool, pool_scale, w_out, g2, w_up, w_down, gf)


def kernel(x, positions, norm_attn_g, w_in, lam_q1, lam_k1, lam_q2, lam_k2, subln_g, w_pool, pool_scale,
           w_out, norm_mlp_g, w_up, w_down, final_norm_g):
    B, S, D = x.shape
    assert D == D_MODEL and w_in.shape[0] == 1, "single-layer block of width 1024"
    lam_init = 0.8 - 0.6 * math.exp(-0.3 * 0)
    bf = jnp.bfloat16
    x2 = x.reshape(B * S, D)
    inv_freq = ROPE_THETA ** (-jnp.arange(0, ROPE_DIM, 2, dtype=jnp.float32) / ROPE_DIM)
    invf = inv_freq.reshape(ROPE_DIM // 2, 1)

    qt, k, vt, u_pool, gates = _inproj(x2, positions, invf, norm_attn_g, w_in[0].astype(bf), B, S)
    a = _attention(qt, k, vt, lam_q1, lam_k1, lam_q2, lam_k2, subln_g, lam_init)
    out = _mix_mlp(x2, a.reshape(B * S, D), gates, u_pool, w_pool[0].astype(bf), pool_scale,
                   w_out[0].astype(bf), norm_mlp_g, w_up[0].astype(bf), w_down[0].astype(bf),
                   final_norm_g.reshape(1, D), S)
    return out.reshape(B, S, D)
```

```python
import functools
import math

import jax
import jax.numpy as jnp
from jax import lax
from jax.experimental import pallas as pl
from jax.experimental.pallas import tpu as pltpu

D_MODEL = 1024
N_HEADS = 8
HEAD_DIM = 64
V_DIM = 128
ROPE_DIM = 16
ROPE_THETA = 500000.0
POOL_WINDOWS = (2, 4, 8, 16)
POOL_GROUP_IN = 128
POOL_GROUP_OUT = 256
POOL_WIDTH = 512
D_FF = 4096
RMS_EPS = 1e-6
LANES = 128
POOL_HALO = 16

QKV_COLS = 3 * D_MODEL
GATE_COL0 = QKV_COLS + POOL_WIDTH
IN_WIDTH = GATE_COL0 + 2 * D_MODEL

TOKEN_TILE = 512
INPROJ_TILE = 1024
ATTN_TILE = 512
STRIP = 256
HEADS_PER_STEP = 4
Q_SCALE = HEAD_DIM ** -0.5 * math.log2(math.e)
VMEM_LIMIT = 56 * 1024 * 1024


def _rms(x, g):
    return x * lax.rsqrt(jnp.mean(x * x, axis=-1, keepdims=True) + RMS_EPS) * g


def _inproj_kernel(x_ref, pos_ref, invf_ref, g_ref, w_ref, qt_ref, k_ref, vt_ref, pool_ref, gate_ref, *,
                   chunk, tiles_per_seq):
    tm = x_ref.shape[0]
    ta = qt_ref.shape[-1]
    half = ROPE_DIM // 2
    h = _rms(x_ref[...], g_ref[...]).astype(jnp.bfloat16)

    step = pl.program_id(0)
    pos = pos_ref[pl.ds(step // tiles_per_seq, 1), :].astype(jnp.float32)
    ang = invf_ref[...] * pos
    cos, sin = jnp.cos(ang), jnp.sin(ang)

    ones = jnp.ones((HEAD_DIM - ROPE_DIM, tm), jnp.float32)
    zeros = jnp.zeros((half, tm), jnp.float32)
    rest0 = jnp.zeros((HEAD_DIM - ROPE_DIM, tm), jnp.float32)
    c_tab = jnp.concatenate([cos, cos, ones] * 2, axis=0).T
    s_lo = jnp.concatenate([-sin, zeros, rest0] * 2, axis=0).T
    s_hi = jnp.concatenate([zeros, sin, rest0] * 2, axis=0).T

    def rope(t):
        return (t * c_tab + pltpu.roll(t, LANES - half, 1) * s_lo + pltpu.roll(t, half, 1) * s_hi)

    def store_transposed(ref, head, t, rotary_scale=None):
        for c in range(tm // ta):
            tt = t[c * ta:(c + 1) * ta].T
            if rotary_scale is not None:
                cc, ss = cos[:, c * ta:(c + 1) * ta], sin[:, c * ta:(c + 1) * ta]
                blocks = [tt[r:r + half] for r in range(0, LANES, half)]
                for b0 in (0, HEAD_DIM // half):
                    lo, hi = blocks[b0], blocks[b0 + 1]
                    blocks[b0], blocks[b0 + 1] = lo * cc - hi * ss, hi * cc + lo * ss
                tt = jnp.concatenate(blocks, axis=0) * rotary_scale
            ref[0, head, c] = tt.astype(ref.dtype)

    starts = list(range(0, IN_WIDTH, chunk))
    order = ([c0 for c0 in starts if c0 >= GATE_COL0] + [c0 for c0 in starts if 2 * D_MODEL <= c0 < QKV_COLS]
             + [c0 for c0 in starts if c0 < 2 * D_MODEL] + [c0 for c0 in starts if QKV_COLS <= c0 < GATE_COL0])
    for c0 in order:
        r = jnp.dot(h, w_ref[:, c0:c0 + chunk], preferred_element_type=jnp.float32)
        if c0 < QKV_COLS:
            for j in range(chunk // LANES):
                slab = (c0 // LANES) + j
                t = r[:, j * LANES:(j + 1) * LANES]
                if slab < N_HEADS:
                    store_transposed(qt_ref, slab, t, rotary_scale=Q_SCALE)
                elif slab < 2 * N_HEADS:
                    k_ref[0, slab - N_HEADS] = rope(t).astype(k_ref.dtype)
                else:
                    store_transposed(vt_ref, slab - 2 * N_HEADS, t)
        elif c0 < GATE_COL0:
            pool_ref[:, c0 - QKV_COLS:c0 - QKV_COLS + chunk] = r
        else:
            gate_ref[:, c0 - GATE_COL0:c0 - GATE_COL0 + chunk] = jax.nn.sigmoid(r).astype(gate_ref.dtype)


def _inproj(x2, pos2, invf, g, w_in, B, S, tm=INPROJ_TILE, ta=ATTN_TILE, chunk=512):
    T = x2.shape[0]
    tiles_per_seq = S // tm
    sub = tm // ta
    const = lambda i: (0, 0)
    bf = jnp.bfloat16
    transposed = pl.BlockSpec((1, N_HEADS, sub, LANES, ta),
                              lambda i: (i // tiles_per_seq, 0, i % tiles_per_seq, 0, 0))
    return pl.pallas_call(
        functools.partial(_inproj_kernel, chunk=chunk, tiles_per_seq=tiles_per_seq),
        grid=(T // tm,),
        in_specs=[
            pl.BlockSpec((tm, D_MODEL), lambda i: (i, 0)),
            pl.BlockSpec((B, tm), lambda i: (0, i % tiles_per_seq)),
            pl.BlockSpec((ROPE_DIM // 2, 1), const),
            pl.BlockSpec((1, D_MODEL), const),
            pl.BlockSpec((D_MODEL, IN_WIDTH), const, pipeline_mode=pl.Buffered(1)),
        ],
        out_specs=[
            transposed,
            pl.BlockSpec((1, N_HEADS, tm, LANES), lambda i: (i // tiles_per_seq, 0, i % tiles_per_seq, 0)),
            transposed,
            pl.BlockSpec((tm, POOL_WIDTH), lambda i: (i, 0)),
            pl.BlockSpec((tm, 2 * D_MODEL), lambda i: (i, 0)),
        ],
        out_shape=[
            jax.ShapeDtypeStruct((B, N_HEADS, S // ta, LANES, ta), bf),
            jax.ShapeDtypeStruct((B, N_HEADS, S, LANES), bf),
            jax.ShapeDtypeStruct((B, N_HEADS, S // ta, LANES, ta), bf),
            jax.ShapeDtypeStruct((T, POOL_WIDTH), jnp.float32),
            jax.ShapeDtypeStruct((T, 2 * D_MODEL), bf),
        ],
        compiler_params=pltpu.CompilerParams(
            dimension_semantics=("arbitrary",), vmem_limit_bytes=VMEM_LIMIT),
        name="inproj",
    )(x2, pos2, invf, g, w_in)


def _attn_kernel(lq1_ref, lk1_ref, lq2_ref, lk2_ref, g_ref, qt_ref, k_ref, vt_ref, o_ref,
                 acc_ref, s_ref, *, lam_init):
    n_heads, n_tiles, _, ta = qt_ref.shape[1:]
    lam = (jnp.exp(jnp.sum(lq1_ref[...] * lk1_ref[...], axis=-1, keepdims=True))
           - jnp.exp(jnp.sum(lq2_ref[...] * lk2_ref[...], axis=-1, keepdims=True))
           + lam_init)
    first_map = lax.broadcasted_iota(jnp.int32, (LANES, STRIP), 0) < HEAD_DIM

    strips_per_map = ta // STRIP
    n_strips = 2 * strips_per_map
    items = [(h, c) for c in range(n_strips) for h in range(n_heads)]
    neg_inf = jnp.full((1, STRIP), -jnp.inf, jnp.float32)

    def scores(qi, j, h, c):
        q0 = (c % strips_per_map) * STRIP
        qt = qt_ref[0, h, qi, :, q0:q0 + STRIP]
        zero = jnp.zeros_like(qt)
        w = jnp.where(first_map, qt, zero) if c < strips_per_map else jnp.where(first_map, zero, qt)
        k0 = pl.multiple_of(j * ta, ta)
        s = jnp.dot(k_ref[0, h, pl.ds(k0, ta), :], w, preferred_element_type=jnp.float32)
        s_ref[h, c] = s
        return jnp.max(s, axis=0, keepdims=True)

    def accumulate(slot, j, h, c, s_max, m_old, l_old, diagonal):
        if diagonal:
            q0 = (c % strips_per_map) * STRIP
            n_keys = q0 + STRIP
            causal = (lax.broadcasted_iota(jnp.int32, (STRIP, STRIP), 0)
                      <= lax.broadcasted_iota(jnp.int32, (STRIP, STRIP), 1))
            s = jnp.concatenate(
                [jnp.where(causal, s_ref[h, c, kb:kb + STRIP, :], -jnp.inf) if kb == q0
                 else s_ref[h, c, kb:kb + STRIP, :] for kb in range(0, n_keys, STRIP)], axis=0)
            s_max = jnp.max(s, axis=0, keepdims=True)
            vt = vt_ref[0, h, j, :, 0:n_keys]
        else:
            s = s_ref[h, c]
            vt = vt_ref[0, h, j]
        m_new = jnp.maximum(m_old, s_max)
        alpha = jnp.exp2(m_old - m_new)
        p = jnp.exp2(s - m_new)
        l_new = alpha * l_old + jnp.sum(p, axis=0, keepdims=True)
        pv = jnp.dot(vt, p.astype(jnp.bfloat16), preferred_element_type=jnp.float32)
        acc_ref[slot, h, c] = alpha * acc_ref[slot, h, c] + pv
        return m_new, l_new

    def finish(slot, qi, h, l):
        q0 = pl.multiple_of(qi * ta, ta)
        for c in range(strips_per_map):
            c2 = strips_per_map + c
            o = acc_ref[slot, h, c] * (1.0 / l[c]) - acc_ref[slot, h, c2] * (lam / l[c2])
            scale = lax.rsqrt(jnp.mean(o * o, axis=0, keepdims=True) + RMS_EPS) * (1.0 - lam_init)
            o = o * scale * g_ref[...]
            o_ref[0, pl.ds(q0 + c * STRIP, STRIP), h * V_DIM:(h + 1) * V_DIM] = o.T.astype(o_ref.dtype)

    def step(slot, qi, j, qi_next, j_next, state, diagonal):
        new, row_sums = [], {}
        for (h, c), (s_max, ml) in zip(items, state):
            second = slot == 1
            m_old = jnp.where(second, ml[1][0], ml[0][0])
            l_old = jnp.where(second, ml[1][1], ml[0][1])
            m, l = accumulate(slot, j, h, c, s_max, m_old, l_old, diagonal)
            ml = ((jnp.where(second, ml[0][0], m), jnp.where(second, ml[0][1], l)),
                  (jnp.where(second, m, ml[1][0]), jnp.where(second, l, ml[1][1])))
            new.append((scores(qi_next, j_next, h, c), ml))
            row_sums[h, c] = l
            if diagonal and all((h, cc) in row_sums for cc in range(n_strips)):
                finish(slot, qi, h, [row_sums[h, cc] for cc in range(n_strips)])
        return tuple(new)

    assert n_tiles % 2 == 0
    n_full = n_tiles - 1

    def tile_pair(a, s_maxes):
        b = n_tiles - 1 - a
        acc_ref[...] = jnp.zeros_like(acc_ref)
        fresh = (neg_inf, jnp.zeros((1, STRIP), jnp.float32))
        state = tuple((s_max, (fresh, fresh)) for s_max in s_maxes)

        def block(t):
            second = t >= a
            return second.astype(jnp.int32), jnp.where(second, b, a), jnp.where(second, t - a, t)

        for t in range(n_full):
            slot, tile, j = block(jnp.int32(t))
            if t + 1 < n_full:
                _, tile_next, j_next = block(jnp.int32(t + 1))
            else:
                tile_next, j_next = a, a
            state = step(slot, tile, j, tile_next, j_next, state, False)
        state = step(0, a, a, b, b, state, True)
        state = step(1, b, b, jnp.minimum(a + 1, n_tiles // 2 - 1), 0, state, True)
        return tuple(st[0] for st in state)

    lax.fori_loop(0, n_tiles // 2, tile_pair, tuple(scores(n_tiles - 1, 0, h, c) for h, c in items))


def _attention(qt, k, vt, lq1, lk1, lq2, lk2, subln_g, lam_init):
    B, H, n_tiles, _, ta = qt.shape
    S = n_tiles * ta
    hs = HEADS_PER_STEP
    vec = lambda n: pl.BlockSpec((1, n), lambda b, h: (0, 0))
    transposed = pl.BlockSpec((1, hs, n_tiles, LANES, ta), lambda b, h: (b, h, 0, 0, 0))
    return pl.pallas_call(
        functools.partial(_attn_kernel, lam_init=lam_init),
        grid=(B, H // hs),
        in_specs=[vec(HEAD_DIM), vec(HEAD_DIM), vec(HEAD_DIM), vec(HEAD_DIM),
                  pl.BlockSpec((V_DIM, 1), lambda b, h: (0, 0)),
                  transposed,
                  pl.BlockSpec((1, hs, S, LANES), lambda b, h: (b, h, 0, 0)),
                  transposed],
        out_specs=pl.BlockSpec((1, S, hs * V_DIM), lambda b, h: (b, 0, h)),
        out_shape=jax.ShapeDtypeStruct((B, S, H * V_DIM), jnp.bfloat16),
        scratch_shapes=[pltpu.VMEM((2, hs, 2 * ta // STRIP, V_DIM, STRIP), jnp.float32),
                        pltpu.VMEM((hs, 2 * ta // STRIP, ta, STRIP), jnp.float32)],
        compiler_params=pltpu.CompilerParams(
            dimension_semantics=("arbitrary", "arbitrary"), vmem_limit_bytes=VMEM_LIMIT),
        name="diff_attn",
    )(lq1, lk1, lq2, lk2, subln_g.reshape(V_DIM, 1), qt, k, vt)


def _mix_mlp_kernel(x_ref, a_ref, gate_ref, up_ref, uprev_ref, wpool_ref, pscale_ref, wout_ref,
                    g2_ref, wup_ref, wdown_ref, gf_ref, o_ref, ext_ref, *, tiles_per_seq, ff_chunk):
    tm = x_ref.shape[0]
    seq_tile = pl.program_id(0) % tiles_per_seq
    bf = jnp.bfloat16

    prev = uprev_ref[...]
    ext_ref[0:POOL_HALO, :] = jnp.where(seq_tile == 0, jnp.zeros_like(prev), prev)
    ext_ref[POOL_HALO:, :] = up_ref[...]
    s_idx = seq_tile * tm + lax.broadcasted_iota(jnp.int32, (tm, 1), 0)
    ys = []
    for g, w in enumerate(POOL_WINDOWS):
        cols = slice(g * POOL_GROUP_IN, (g + 1) * POOL_GROUP_IN)
        u = ext_ref[POOL_HALO:, cols]
        win = u
        for j in range(1, w):
            win = win + ext_ref[POOL_HALO - j:POOL_HALO - j + tm, cols]
        cnt = jnp.minimum(s_idx + 1, w).astype(jnp.float32)
        d = win / cnt - u
        ys.append(jnp.dot(d.astype(bf), wpool_ref[g], preferred_element_type=jnp.float32))
    p = jnp.concatenate(ys, axis=-1) * pscale_ref[...]

    merged = (gate_ref[:, :D_MODEL].astype(jnp.float32) * a_ref[...].astype(jnp.float32)
              + gate_ref[:, D_MODEL:].astype(jnp.float32) * p)
    x1 = x_ref[...] + jnp.dot(merged.astype(bf), wout_ref[...], preferred_element_type=jnp.float32)

    h2 = _rms(x1, g2_ref[...]).astype(bf)
    acc = jnp.zeros((tm, D_MODEL), jnp.float32)
    for c0 in range(0, D_FF, ff_chunk):
        z = jnp.dot(h2, wup_ref[:, c0:c0 + ff_chunk], preferred_element_type=jnp.float32)
        z = jnp.square(jnp.maximum(z, 0.0)).astype(bf)
        acc = acc + jnp.dot(z, wdown_ref[c0:c0 + ff_chunk, :], preferred_element_type=jnp.float32)
    o_ref[...] = _rms(x1 + acc, gf_ref[...])


def _mix_mlp(x2, a2, gates, u_pool, w_pool, pool_scale, w_out, g2, w_up, w_down, gf, S, tm=TOKEN_TILE, ff_chunk=512):
    T = x2.shape[0]
    tiles_per_seq = S // tm
    halo_blocks = tm // POOL_HALO
    const2 = lambda i: (0, 0)
    resident = functools.partial(pl.BlockSpec, pipeline_mode=pl.Buffered(1))
    return pl.pallas_call(
        functools.partial(_mix_mlp_kernel, tiles_per_seq=tiles_per_seq, ff_chunk=ff_chunk),
        grid=(T // tm,),
        in_specs=[
            pl.BlockSpec((tm, D_MODEL), lambda i: (i, 0)),
            pl.BlockSpec((tm, D_MODEL), lambda i: (i, 0)),
            pl.BlockSpec((tm, 2 * D_MODEL), lambda i: (i, 0)),
            pl.BlockSpec((tm, POOL_WIDTH), lambda i: (i, 0)),
            pl.BlockSpec((POOL_HALO, POOL_WIDTH), lambda i: (jnp.maximum(i * halo_blocks - 1, 0), 0)),
            resident((len(POOL_WINDOWS), POOL_GROUP_IN, POOL_GROUP_OUT), lambda i: (0, 0, 0)),
            pl.BlockSpec((1, D_MODEL), const2),
            resident((D_MODEL, D_MODEL), const2),
            pl.BlockSpec((1, D_MODEL), const2),
            resident((D_MODEL, D_FF), const2),
            resident((D_FF, D_MODEL), const2),
            pl.BlockSpec((1, D_MODEL), const2),
        ],
        out_specs=pl.BlockSpec((tm, D_MODEL), lambda i: (i, 0)),
        out_shape=jax.ShapeDtypeStruct((T, D_MODEL), jnp.float32),
        scratch_shapes=[pltpu.VMEM((tm + POOL_HALO, POOL_WIDTH), jnp.float32)],
        compiler_params=pltpu.CompilerParams(
            dimension_semantics=("arbitrary",), vmem_limit_bytes=VMEM_LIMIT),
        name="mix_mlp",
    )(x2, a2, gates, u_pool, u_pool, w_pool, pool_scale, w_out, g2, w_up, w_down, gf)


def kernel(x, positions, norm_attn_g, w_in, lam_q1, lam_k1, lam_q2, lam_k2, subln_g, w_pool, pool_scale,
           w_out, norm_mlp_g, w_up, w_down, final_norm_g):
    B, S, D = x.shape
    assert D == D_MODEL and w_in.shape[0] == 1, "single-layer block of width 1024"
    lam_init = 0.8 - 0.6 * math.exp(-0.3 * 0)
    bf = jnp.bfloat16
    x2 = x.reshape(B * S, D)
    inv_freq = ROPE_THETA ** (-jnp.arange(0, ROPE_DIM, 2, dtype=jnp.float32) / ROPE_DIM)
    invf = inv_freq.reshape(ROPE_DIM // 2, 1)

    qt, k, vt, u_pool, gates = _inproj(x2, positions, invf, norm_attn_g, w_in[0].astype(bf), B, S)
    a = _attention(qt, k, vt, lam_q1, lam_k1, lam_q2, lam_k2, subln_g, lam_init)
    out = _mix_mlp(x2, a.reshape(B * S, D), gates, u_pool, w_pool[0].astype(bf), pool_scale,
                   w_out[0].astype(bf), norm_mlp_g, w_up[0].astype(bf), w_down[0].astype(bf),
                   final_norm_g.reshape(1, D), S)
    return out.reshape(B, S, D)
```

```python
import functools
import math

import jax
import jax.numpy as jnp
from jax import lax
from jax.experimental import pallas as pl
from jax.experimental.pallas import tpu as pltpu

D_MODEL = 1024
N_HEADS = 8
HEAD_DIM = 64
V_DIM = 128
ROPE_DIM = 16
ROPE_THETA = 500000.0
POOL_WINDOWS = (2, 4, 8, 16)
POOL_GROUP_IN = 128
POOL_GROUP_OUT = 256
POOL_WIDTH = 512
D_FF = 4096
RMS_EPS = 1e-6
LANES = 128
POOL_HALO = 16

QKV_COLS = 3 * D_MODEL
GATE_COL0 = QKV_COLS + POOL_WIDTH
IN_WIDTH = GATE_COL0 + 2 * D_MODEL

TOKEN_TILE = 512
INPROJ_TILE = 1024
ATTN_TILE = 512
STRIP = 256
HEADS_PER_STEP = 4
Q_SCALE = HEAD_DIM ** -0.5 * math.log2(math.e)
VMEM_LIMIT = 56 * 1024 * 1024


def _rms(x, g):
    return x * lax.rsqrt(jnp.mean(x * x, axis=-1, keepdims=True) + RMS_EPS) * g


def _inproj_kernel(x_ref, pos_ref, invf_ref, g_ref, w_ref, qt_ref, k_ref, vt_ref, pool_ref, gate_ref, *,
                   chunk, tiles_per_seq):
    tm = x_ref.shape[0]
    ta = qt_ref.shape[-1]
    half = ROPE_DIM // 2
    h = _rms(x_ref[...], g_ref[...]).astype(jnp.bfloat16)

    step = pl.program_id(0)
    pos = pos_ref[pl.ds(step // tiles_per_seq, 1), :].astype(jnp.float32)
    ang = invf_ref[...] * pos
    cos, sin = jnp.cos(ang), jnp.sin(ang)

    ones = jnp.ones((HEAD_DIM - ROPE_DIM, tm), jnp.float32)
    zeros = jnp.zeros((half, tm), jnp.float32)
    rest0 = jnp.zeros((HEAD_DIM - ROPE_DIM, tm), jnp.float32)
    c_tab = jnp.concatenate([cos, cos, ones] * 2, axis=0).T
    s_lo = jnp.concatenate([-sin, zeros, rest0] * 2, axis=0).T
    s_hi = jnp.concatenate([zeros, sin, rest0] * 2, axis=0).T

    def rope(t):
        return (t * c_tab + pltpu.roll(t, LANES - half, 1) * s_lo + pltpu.roll(t, half, 1) * s_hi)

    def store_transposed(ref, head, t, rotary_scale=None):
        for c in range(tm // ta):
            tt = t[c * ta:(c + 1) * ta].T
            if rotary_scale is not None:
                cc, ss = cos[:, c * ta:(c + 1) * ta], sin[:, c * ta:(c + 1) * ta]
                blocks = [tt[r:r + half] for r in range(0, LANES, half)]
                for b0 in (0, HEAD_DIM // half):
                    lo, hi = blocks[b0], blocks[b0 + 1]
                    blocks[b0], blocks[b0 + 1] = lo * cc - hi * ss, hi * cc + lo * ss
                tt = jnp.concatenate(blocks, axis=0) * rotary_scale
            ref[0, head, c] = tt.astype(ref.dtype)

    starts = list(range(0, IN_WIDTH, chunk))
    order = ([c0 for c0 in starts if c0 >= GATE_COL0] + [c0 for c0 in starts if 2 * D_MODEL <= c0 < QKV_COLS]
             + [c0 for c0 in starts if c0 < 2 * D_MODEL] + [c0 for c0 in starts if QKV_COLS <= c0 < GATE_COL0])
    for c0 in order:
        r = jnp.dot(h, w_ref[:, c0:c0 + chunk], preferred_element_type=jnp.float32)
        if c0 < QKV_COLS:
            for j in range(chunk // LANES):
                slab = (c0 // LANES) + j
                t = r[:, j * LANES:(j + 1) * LANES]
                if slab < N_HEADS:
                    store_transposed(qt_ref, slab, t, rotary_scale=Q_SCALE)
                elif slab < 2 * N_HEADS:
                    k_ref[0, slab - N_HEADS] = rope(t).astype(k_ref.dtype)
                else:
                    store_transposed(vt_ref, slab - 2 * N_HEADS, t)
        elif c0 < GATE_COL0:
            pool_ref[:, c0 - QKV_COLS:c0 - QKV_COLS + chunk] = r
        else:
            gate_ref[:, c0 - GATE_COL0:c0 - GATE_COL0 + chunk] = jax.nn.sigmoid(r).astype(gate_ref.dtype)


def _inproj(x2, pos2, invf, g, w_in, B, S, tm=INPROJ_TILE, ta=ATTN_TILE, chunk=512):
    T = x2.shape[0]
    tiles_per_seq = S // tm
    sub = tm // ta
    const = lambda i: (0, 0)
    bf = jnp.bfloat16
    transposed = pl.BlockSpec((1, N_HEADS, sub, LANES, ta),
                              lambda i: (i // tiles_per_seq, 0, i % tiles_per_seq, 0, 0))
    return pl.pallas_call(
        functools.partial(_inproj_kernel, chunk=chunk, tiles_per_seq=tiles_per_seq),
        grid=(T // tm,),
        in_specs=[
            pl.BlockSpec((tm, D_MODEL), lambda i: (i, 0)),
            pl.BlockSpec((B, tm), lambda i: (0, i % tiles_per_seq)),
            pl.BlockSpec((ROPE_DIM // 2, 1), const),
            pl.BlockSpec((1, D_MODEL), const),
            pl.BlockSpec((D_MODEL, IN_WIDTH), const, pipeline_mode=pl.Buffered(1)),
        ],
        out_specs=[
            transposed,
            pl.BlockSpec((1, N_HEADS, tm, LANES), lambda i: (i // tiles_per_seq, 0, i % tiles_per_seq, 0)),
            transposed,
            pl.BlockSpec((tm, POOL_WIDTH), lambda i: (i, 0)),
            pl.BlockSpec((tm, 2 * D_MODEL), lambda i: (i, 0)),
        ],
        out_shape=[
            jax.ShapeDtypeStruct((B, N_HEADS, S // ta, LANES, ta), bf),
            jax.ShapeDtypeStruct((B, N_HEADS, S, LANES), bf),
            jax.ShapeDtypeStruct((B, N_HEADS, S // ta, LANES, ta), bf),
            jax.ShapeDtypeStruct((T, POOL_WIDTH), jnp.float32),
            jax.ShapeDtypeStruct((T, 2 * D_MODEL), bf),
        ],
        compiler_params=pltpu.CompilerParams(
            dimension_semantics=("arbitrary",), vmem_limit_bytes=VMEM_LIMIT),
        name="inproj",
    )(x2, pos2, invf, g, w_in)


def _attn_kernel(lq1_ref, lk1_ref, lq2_ref, lk2_ref, g_ref, qt_ref, k_ref, vt_ref, o_ref,
                 acc_ref, s_ref, *, lam_init):
    n_heads, n_tiles, _, ta = qt_ref.shape[1:]
    lam = (jnp.exp(jnp.sum(lq1_ref[...] * lk1_ref[...], axis=-1, keepdims=True))
           - jnp.exp(jnp.sum(lq2_ref[...] * lk2_ref[...], axis=-1, keepdims=True))
           + lam_init)
    first_map = lax.broadcasted_iota(jnp.int32, (LANES, STRIP), 0) < HEAD_DIM

    strips_per_map = ta // STRIP
    n_strips = 2 * strips_per_map
    items = [(h, c) for c in range(n_strips) for h in range(n_heads)]
    neg_inf = jnp.full((1, STRIP), -jnp.inf, jnp.float32)

    def scores(qi, j, h, c, diagonal=False):
        q0 = (c % strips_per_map) * STRIP
        qt = qt_ref[0, h, qi, :, q0:q0 + STRIP]
        zero = jnp.zeros_like(qt)
        w = jnp.where(first_map, qt, zero) if c < strips_per_map else jnp.where(first_map, zero, qt)
        k0 = pl.multiple_of(j * ta, ta)
        n_keys = q0 + STRIP if diagonal else ta
        s = jnp.dot(k_ref[0, h, pl.ds(k0, n_keys), :], w, preferred_element_type=jnp.float32)
        s_ref[h, c, 0:n_keys, :] = s
        return jnp.max(s, axis=0, keepdims=True)

    def accumulate(slot, j, h, c, s_max, m_old, l_old, diagonal):
        if diagonal:
            q0 = (c % strips_per_map) * STRIP
            n_keys = q0 + STRIP
            causal = (lax.broadcasted_iota(jnp.int32, (STRIP, STRIP), 0)
                      <= lax.broadcasted_iota(jnp.int32, (STRIP, STRIP), 1))
            s = jnp.concatenate(
                [jnp.where(causal, s_ref[h, c, kb:kb + STRIP, :], -jnp.inf) if kb == q0
                 else s_ref[h, c, kb:kb + STRIP, :] for kb in range(0, n_keys, STRIP)], axis=0)
            s_max = jnp.max(s, axis=0, keepdims=True)
            vt = vt_ref[0, h, j, :, 0:n_keys]
        else:
            s = s_ref[h, c]
            vt = vt_ref[0, h, j]
        m_new = jnp.maximum(m_old, s_max)
        alpha = jnp.exp2(m_old - m_new)
        p = jnp.exp2(s - m_new)
        l_new = alpha * l_old + jnp.sum(p, axis=0, keepdims=True)
        pv = jnp.dot(vt, p.astype(jnp.bfloat16), preferred_element_type=jnp.float32)
        acc_ref[slot, h, c] = alpha * acc_ref[slot, h, c] + pv
        return m_new, l_new

    def finish(slot, qi, h, l):
        q0 = pl.multiple_of(qi * ta, ta)
        for c in range(strips_per_map):
            c2 = strips_per_map + c
            o = acc_ref[slot, h, c] * (1.0 / l[c]) - acc_ref[slot, h, c2] * (lam / l[c2])
            scale = lax.rsqrt(jnp.mean(o * o, axis=0, keepdims=True) + RMS_EPS) * (1.0 - lam_init)
            o = o * scale * g_ref[...]
            o_ref[0, pl.ds(q0 + c * STRIP, STRIP), h * V_DIM:(h + 1) * V_DIM] = o.T.astype(o_ref.dtype)

    def step(slot, qi, j, qi_next, j_next, state, diagonal, diagonal_next=False):
        new, row_sums = [], {}
        for (h, c), (s_max, ml) in zip(items, state):
            second = slot == 1
            m_old = jnp.where(second, ml[1][0], ml[0][0])
            l_old = jnp.where(second, ml[1][1], ml[0][1])
            m, l = accumulate(slot, j, h, c, s_max, m_old, l_old, diagonal)
            ml = ((jnp.where(second, ml[0][0], m), jnp.where(second, ml[0][1], l)),
                  (jnp.where(second, m, ml[1][0]), jnp.where(second, l, ml[1][1])))
            new.append((scores(qi_next, j_next, h, c, diagonal_next), ml))
            row_sums[h, c] = l
            if diagonal and all((h, cc) in row_sums for cc in range(n_strips)):
                finish(slot, qi, h, [row_sums[h, cc] for cc in range(n_strips)])
        return tuple(new)

    assert n_tiles % 2 == 0
    n_full = n_tiles - 1

    def tile_pair(a, s_maxes):
        b = n_tiles - 1 - a
        acc_ref[...] = jnp.zeros_like(acc_ref)
        fresh = (neg_inf, jnp.zeros((1, STRIP), jnp.float32))
        state = tuple((s_max, (fresh, fresh)) for s_max in s_maxes)

        def block(t):
            second = t >= a
            return second.astype(jnp.int32), jnp.where(second, b, a), jnp.where(second, t - a, t)

        for t in range(n_full):
            slot, tile, j = block(jnp.int32(t))
            if t + 1 < n_full:
                _, tile_next, j_next = block(jnp.int32(t + 1))
            else:
                tile_next, j_next = a, a
            state = step(slot, tile, j, tile_next, j_next, state, False, diagonal_next=t + 1 == n_full)
        state = step(0, a, a, b, b, state, True, diagonal_next=True)
        state = step(1, b, b, jnp.minimum(a + 1, n_tiles // 2 - 1), 0, state, True)
        return tuple(st[0] for st in state)

    lax.fori_loop(0, n_tiles // 2, tile_pair, tuple(scores(n_tiles - 1, 0, h, c) for h, c in items))


def _attention(qt, k, vt, lq1, lk1, lq2, lk2, subln_g, lam_init):
    B, H, n_tiles, _, ta = qt.shape
    S = n_tiles * ta
    hs = HEADS_PER_STEP
    vec = lambda n: pl.BlockSpec((1, n), lambda b, h: (0, 0))
    transposed = pl.BlockSpec((1, hs, n_tiles, LANES, ta), lambda b, h: (b, h, 0, 0, 0))
    return pl.pallas_call(
        functools.partial(_attn_kernel, lam_init=lam_init),
        grid=(B, H // hs),
        in_specs=[vec(HEAD_DIM), vec(HEAD_DIM), vec(HEAD_DIM), vec(HEAD_DIM),
                  pl.BlockSpec((V_DIM, 1), lambda b, h: (0, 0)),
                  transposed,
                  pl.BlockSpec((1, hs, S, LANES), lambda b, h: (b, h, 0, 0)),
                  transposed],
        out_specs=pl.BlockSpec((1, S, hs * V_DIM), lambda b, h: (b, 0, h)),
        out_shape=jax.ShapeDtypeStruct((B, S, H * V_DIM), jnp.bfloat16),
        scratch_shapes=[pltpu.VMEM((2, hs, 2 * ta // STRIP, V_DIM, STRIP), jnp.float32),
                        pltpu.VMEM((hs, 2 * ta // STRIP, ta, STRIP), jnp.float32)],
        compiler_params=pltpu.CompilerParams(
            dimension_semantics=("arbitrary", "arbitrary"), vmem_limit_bytes=VMEM_LIMIT),
        name="diff_attn",
    )(lq1, lk1, lq2, lk2, subln_g.reshape(V_DIM, 1), qt, k, vt)


def _mix_mlp_kernel(x_ref, a_ref, gate_ref, up_ref, uprev_ref, wpool_ref, pscale_ref, wout_ref,
                    g2_ref, wup_ref, wdown_ref, gf_ref, o_ref, ext_ref, *, tiles_per_seq, ff_chunk):
    tm = x_ref.shape[0]
    seq_tile = pl.program_id(0) % tiles_per_seq
    bf = jnp.bfloat16

    prev = uprev_ref[...]
    ext_ref[0:POOL_HALO, :] = jnp.where(seq_tile == 0, jnp.zeros_like(prev), prev)
    ext_ref[POOL_HALO:, :] = up_ref[...]
    s_idx = seq_tile * tm + lax.broadcasted_iota(jnp.int32, (tm, 1), 0)
    ys = []
    for g, w in enumerate(POOL_WINDOWS):
        cols = slice(g * POOL_GROUP_IN, (g + 1) * POOL_GROUP_IN)
        u = ext_ref[POOL_HALO:, cols]
        win = u
        for j in range(1, w):
            win = win + ext_ref[POOL_HALO - j:POOL_HALO - j + tm, cols]
        cnt = jnp.minimum(s_idx + 1, w).astype(jnp.float32)
        d = win / cnt - u
        ys.append(jnp.dot(d.astype(bf), wpool_ref[g], preferred_element_type=jnp.float32))
    p = jnp.concatenate(ys, axis=-1) * pscale_ref[...]

    merged = (gate_ref[:, :D_MODEL].astype(jnp.float32) * a_ref[...].astype(jnp.float32)
              + gate_ref[:, D_MODEL:].astype(jnp.float32) * p)
    x1 = x_ref[...] + jnp.dot(merged.astype(bf), wout_ref[...], preferred_element_type=jnp.float32)

    h2 = _rms(x1, g2_ref[...]).astype(bf)
    acc = jnp.zeros((tm, D_MODEL), jnp.float32)
    for c0 in range(0, D_FF, ff_chunk):
        z = jnp.dot(h2, wup_ref[:, c0:c0 + ff_chunk], preferred_element_type=jnp.float32)
        z = jnp.square(jnp.maximum(z, 0.0)).astype(bf)
        acc = acc + jnp.dot(z, wdown_ref[c0:c0 + ff_chunk, :], preferred_element_type=jnp.float32)
    o_ref[...] = _rms(x1 + acc, gf_ref[...])


def _mix_mlp(x2, a2, gates, u_pool, w_pool, pool_scale, w_out, g2, w_up, w_down, gf, S, tm=TOKEN_TILE, ff_chunk=512):
    T = x2.shape[0]
    tiles_per_seq = S // tm
    halo_blocks = tm // POOL_HALO
    const2 = lambda i: (0, 0)
    resident = functools.partial(pl.BlockSpec, pipeline_mode=pl.Buffered(1))
    return pl.pallas_call(
        functools.partial(_mix_mlp_kernel, tiles_per_seq=tiles_per_seq, ff_chunk=ff_chunk),
        grid=(T // tm,),
        in_specs=[
            pl.BlockSpec((tm, D_MODEL), lambda i: (i, 0)),
            pl.BlockSpec((tm, D_MODEL), lambda i: (i, 0)),
            pl.BlockSpec((tm, 2 * D_MODEL), lambda i: (i, 0)),
            pl.BlockSpec((tm, POOL_WIDTH), lambda i: (i, 0)),
            pl.BlockSpec((POOL_HALO, POOL_WIDTH), lambda i: (jnp.maximum(i * halo_blocks - 1, 0), 0)),
            resident((len(POOL_WINDOWS), POOL_GROUP_IN, POOL_GROUP_OUT), lambda i: (0, 0, 0)),
            pl.BlockSpec((1, D_MODEL), const2),
            resident((D_MODEL, D_MODEL), const2),
            pl.BlockSpec((1, D_MODEL), const2),
            resident((D_MODEL, D_FF), const2),
            resident((D_FF, D_MODEL), const2),
            pl.BlockSpec((1, D_MODEL), const2),
        ],
        out_specs=pl.BlockSpec((tm, D_MODEL), lambda i: (i, 0)),
        out_shape=jax.ShapeDtypeStruct((T, D_MODEL), jnp.float32),
        scratch_shapes=[pltpu.VMEM((tm + POOL_HALO, POOL_WIDTH), jnp.float32)],
        compiler_params=pltpu.CompilerParams(
            dimension_semantics=("arbitrary",), vmem_limit_bytes=VMEM_LIMIT),
        name="mix_mlp",
    )(x2, a2, gates, u_pool, u_pool, w_pool, pool_scale, w_out, g2, w_up, w_down, gf)


def kernel(x, positions, norm_attn_g, w_in, lam_q1, lam_k1, lam_q2, lam_k2, subln_g, w_pool, pool_scale,
           w_out, norm_mlp_g, w_up, w_down, final_norm_g):
    B, S, D = x.shape
    assert D == D_MODEL and w_in.shape[0] == 1, "single-layer block of width 1024"
    lam_init = 0.8 - 0.6 * math.exp(-0.3 * 0)
    bf = jnp.bfloat16
    x2 = x.reshape(B * S, D)
    inv_freq = ROPE_THETA ** (-jnp.arange(0, ROPE_DIM, 2, dtype=jnp.float32) / ROPE_DIM)
    invf = inv_freq.reshape(ROPE_DIM // 2, 1)

    qt, k, vt, u_pool, gates = _inproj(x2, positions, invf, norm_attn_g, w_in[0].astype(bf), B, S)
    a = _attention(qt, k, vt, lam_q1, lam_k1, lam_q2, lam_k2, subln_g, lam_init)
    out = _mix_mlp(x2, a.reshape(B * S, D), gates, u_pool, w_pool[0].astype(bf), pool_scale,
                   w_out[0].astype(bf), norm_mlp_g, w_up[0].astype(bf), w_down[0].astype(bf),
                   final_norm_g.reshape(1, D), S)
    return out.reshape(B, S, D)
```

```python
import functools
import math

import jax
import jax.numpy as jnp
from jax import lax
from jax.experimental import pallas as pl
from jax.experimental.pallas import tpu as pltpu

D_MODEL = 1024
N_HEADS = 8
HEAD_DIM = 64
V_DIM = 128
ROPE_DIM = 16
ROPE_THETA = 500000.0
POOL_WINDOWS = (2, 4, 8, 16)
POOL_GROUP_IN = 128
POOL_GROUP_OUT = 256
POOL_WIDTH = 512
D_FF = 4096
RMS_EPS = 1e-6
LANES = 128
POOL_HALO = 16

QKV_COLS = 3 * D_MODEL
GATE_COL0 = QKV_COLS + POOL_WIDTH
IN_WIDTH = GATE_COL0 + 2 * D_MODEL

TOKEN_TILE = 512
INPROJ_TILE = 1024
ATTN_TILE = 512
STRIP = 256
HEADS_PER_STEP = 4
Q_SCALE = HEAD_DIM ** -0.5 * math.log2(math.e)
VMEM_LIMIT = 56 * 1024 * 1024


def _rms(x, g):
    return x * lax.rsqrt(jnp.mean(x * x, axis=-1, keepdims=True) + RMS_EPS) * g


def _inproj_kernel(x_ref, pos_ref, invf_ref, g_ref, w_ref, qt_ref, k_ref, vt_ref, pool_ref, gate_ref, *,
                   chunk, tiles_per_seq):
    tm = x_ref.shape[0]
    ta = qt_ref.shape[-1]
    half = ROPE_DIM // 2
    h = _rms(x_ref[...], g_ref[...]).astype(jnp.bfloat16)

    step = pl.program_id(0)
    pos = pos_ref[pl.ds(step // tiles_per_seq, 1), :].astype(jnp.float32)
    ang = invf_ref[...] * pos
    cos, sin = jnp.cos(ang), jnp.sin(ang)

    ones = jnp.ones((HEAD_DIM - ROPE_DIM, tm), jnp.float32)
    zeros = jnp.zeros((half, tm), jnp.float32)
    rest0 = jnp.zeros((HEAD_DIM - ROPE_DIM, tm), jnp.float32)
    c_tab = jnp.concatenate([cos, cos, ones] * 2, axis=0).T
    s_lo = jnp.concatenate([-sin, zeros, rest0] * 2, axis=0).T
    s_hi = jnp.concatenate([zeros, sin, rest0] * 2, axis=0).T

    def rope(t):
        return (t * c_tab + pltpu.roll(t, LANES - half, 1) * s_lo + pltpu.roll(t, half, 1) * s_hi)

    def store_transposed(ref, head, t, rotary_scale=None):
        for c in range(tm // ta):
            tt = t[c * ta:(c + 1) * ta].T
            if rotary_scale is not None:
                cc, ss = cos[:, c * ta:(c + 1) * ta], sin[:, c * ta:(c + 1) * ta]
                blocks = [tt[r:r + half] for r in range(0, LANES, half)]
                for b0 in (0, HEAD_DIM // half):
                    lo, hi = blocks[b0], blocks[b0 + 1]
                    blocks[b0], blocks[b0 + 1] = lo * cc - hi * ss, hi * cc + lo * ss
                tt = jnp.concatenate(blocks, axis=0) * rotary_scale
            ref[0, head, c] = tt.astype(ref.dtype)

    starts = list(range(0, IN_WIDTH, chunk))
    order = ([c0 for c0 in starts if c0 >= GATE_COL0] + [c0 for c0 in starts if 2 * D_MODEL <= c0 < QKV_COLS]
             + [c0 for c0 in starts if c0 < 2 * D_MODEL] + [c0 for c0 in starts if QKV_COLS <= c0 < GATE_COL0])
    for c0 in order:
        r = jnp.dot(h, w_ref[:, c0:c0 + chunk], preferred_element_type=jnp.float32)
        if c0 < QKV_COLS:
            for j in range(chunk // LANES):
                slab = (c0 // LANES) + j
                t = r[:, j * LANES:(j + 1) * LANES]
                if slab < N_HEADS:
                    store_transposed(qt_ref, slab, t, rotary_scale=Q_SCALE)
                elif slab < 2 * N_HEADS:
                    k_ref[0, slab - N_HEADS] = rope(t).astype(k_ref.dtype)
                else:
                    store_transposed(vt_ref, slab - 2 * N_HEADS, t)
        elif c0 < GATE_COL0:
            pool_ref[:, c0 - QKV_COLS:c0 - QKV_COLS + chunk] = r
        else:
            gate_ref[:, c0 - GATE_COL0:c0 - GATE_COL0 + chunk] = jax.nn.sigmoid(r).astype(gate_ref.dtype)


def _inproj(x2, pos2, invf, g, w_in, B, S, tm=INPROJ_TILE, ta=ATTN_TILE, chunk=512):
    T = x2.shape[0]
    tiles_per_seq = S // tm
    sub = tm // ta
    const = lambda i: (0, 0)
    bf = jnp.bfloat16
    transposed = pl.BlockSpec((1, N_HEADS, sub, LANES, ta),
                              lambda i: (i // tiles_per_seq, 0, i % tiles_per_seq, 0, 0))
    return pl.pallas_call(
        functools.partial(_inproj_kernel, chunk=chunk, tiles_per_seq=tiles_per_seq),
        grid=(T // tm,),
        in_specs=[
            pl.BlockSpec((tm, D_MODEL), lambda i: (i, 0)),
            pl.BlockSpec((B, tm), lambda i: (0, i % tiles_per_seq)),
            pl.BlockSpec((ROPE_DIM // 2, 1), const),
            pl.BlockSpec((1, D_MODEL), const),
            pl.BlockSpec((D_MODEL, IN_WIDTH), const, pipeline_mode=pl.Buffered(1)),
        ],
        out_specs=[
            transposed,
            pl.BlockSpec((1, N_HEADS, tm, LANES), lambda i: (i // tiles_per_seq, 0, i % tiles_per_seq, 0)),
            transposed,
            pl.BlockSpec((tm, POOL_WIDTH), lambda i: (i, 0)),
            pl.BlockSpec((tm, 2 * D_MODEL), lambda i: (i, 0)),
        ],
        out_shape=[
            jax.ShapeDtypeStruct((B, N_HEADS, S // ta, LANES, ta), bf),
            jax.ShapeDtypeStruct((B, N_HEADS, S, LANES), bf),
            jax.ShapeDtypeStruct((B, N_HEADS, S // ta, LANES, ta), bf),
            jax.ShapeDtypeStruct((T, POOL_WIDTH), jnp.float32),
            jax.ShapeDtypeStruct((T, 2 * D_MODEL), bf),
        ],
        compiler_params=pltpu.CompilerParams(
            dimension_semantics=("arbitrary",), vmem_limit_bytes=VMEM_LIMIT),
        name="inproj",
    )(x2, pos2, invf, g, w_in)


def _attn_kernel(lq1_ref, lk1_ref, lq2_ref, lk2_ref, g_ref, qt_ref, k_ref, vt_ref, o_ref,
                 acc_ref, s_ref, *, lam_init):
    n_heads, n_tiles, _, ta = qt_ref.shape[1:]
    lam = (jnp.exp(jnp.sum(lq1_ref[...] * lk1_ref[...], axis=-1, keepdims=True))
           - jnp.exp(jnp.sum(lq2_ref[...] * lk2_ref[...], axis=-1, keepdims=True))
           + lam_init)
    first_map = lax.broadcasted_iota(jnp.int32, (LANES, STRIP), 0) < HEAD_DIM

    strips_per_map = ta // STRIP
    n_strips = 2 * strips_per_map
    items = [(h, c) for c in range(n_strips) for h in range(n_heads)]
    neg_inf = jnp.full((1, STRIP), -jnp.inf, jnp.float32)

    def scores(qi, j, h, c, diagonal=False):
        q0 = (c % strips_per_map) * STRIP
        qt = qt_ref[0, h, qi, :, q0:q0 + STRIP]
        zero = jnp.zeros_like(qt)
        w = jnp.where(first_map, qt, zero) if c < strips_per_map else jnp.where(first_map, zero, qt)
        k0 = pl.multiple_of(j * ta, ta)
        n_keys = q0 + STRIP if diagonal else ta
        s = jnp.dot(k_ref[0, h, pl.ds(k0, n_keys), :], w, preferred_element_type=jnp.float32)
        s_ref[h, c, 0:n_keys, :] = s
        return None if diagonal else jnp.max(s, axis=0, keepdims=True)

    def accumulate(slot, j, h, c, s_max, m_old, l_old, diagonal):
        if diagonal:
            q0 = (c % strips_per_map) * STRIP
            n_keys = q0 + STRIP
            causal = (lax.broadcasted_iota(jnp.int32, (STRIP, STRIP), 0)
                      <= lax.broadcasted_iota(jnp.int32, (STRIP, STRIP), 1))
            s = jnp.concatenate(
                [jnp.where(causal, s_ref[h, c, kb:kb + STRIP, :], -jnp.inf) if kb == q0
                 else s_ref[h, c, kb:kb + STRIP, :] for kb in range(0, n_keys, STRIP)], axis=0)
            s_max = jnp.max(s, axis=0, keepdims=True)
            vt = vt_ref[0, h, j, :, 0:n_keys]
        else:
            s = s_ref[h, c]
            vt = vt_ref[0, h, j]
        m_new = jnp.maximum(m_old, s_max)
        alpha = jnp.exp2(m_old - m_new)
        p = jnp.exp2(s - m_new)
        l_new = alpha * l_old + jnp.sum(p, axis=0, keepdims=True)
        pv = jnp.dot(vt, p.astype(jnp.bfloat16), preferred_element_type=jnp.float32)
        acc_ref[slot, h, c] = alpha * acc_ref[slot, h, c] + pv
        return m_new, l_new

    def finish(slot, qi, h, l):
        q0 = pl.multiple_of(qi * ta, ta)
        for c in range(strips_per_map):
            c2 = strips_per_map + c
            o = acc_ref[slot, h, c] * (1.0 / l[c]) - acc_ref[slot, h, c2] * (lam / l[c2])
            scale = lax.rsqrt(jnp.mean(o * o, axis=0, keepdims=True) + RMS_EPS) * (1.0 - lam_init)
            o = o * scale * g_ref[...]
            o_ref[0, pl.ds(q0 + c * STRIP, STRIP), h * V_DIM:(h + 1) * V_DIM] = o.T.astype(o_ref.dtype)

    def step(slot, qi, j, qi_next, j_next, state, diagonal, diagonal_next=False):
        new, row_sums = [], {}
        for (h, c), (s_max, ml) in zip(items, state):
            second = slot == 1
            m_old = jnp.where(second, ml[1][0], ml[0][0])
            l_old = jnp.where(second, ml[1][1], ml[0][1])
            m, l = accumulate(slot, j, h, c, s_max, m_old, l_old, diagonal)
            ml = ((jnp.where(second, ml[0][0], m), jnp.where(second, ml[0][1], l)),
                  (jnp.where(second, m, ml[1][0]), jnp.where(second, l, ml[1][1])))
            new.append((scores(qi_next, j_next, h, c, diagonal_next), ml))
            row_sums[h, c] = l
            if diagonal and all((h, cc) in row_sums for cc in range(n_strips)):
                finish(slot, qi, h, [row_sums[h, cc] for cc in range(n_strips)])
        return tuple(new)

    assert n_tiles % 2 == 0
    n_full = n_tiles - 1

    def tile_pair(a, s_maxes):
        b = n_tiles - 1 - a
        acc_ref[...] = jnp.zeros_like(acc_ref)
        fresh = (neg_inf, jnp.zeros((1, STRIP), jnp.float32))
        state = tuple((s_max, (fresh, fresh)) for s_max in s_maxes)

        def block(t):
            second = t >= a
            return second.astype(jnp.int32), jnp.where(second, b, a), jnp.where(second, t - a, t)

        for t in range(n_full):
            slot, tile, j = block(jnp.int32(t))
            if t + 1 < n_full:
                _, tile_next, j_next = block(jnp.int32(t + 1))
            else:
                tile_next, j_next = a, a
            state = step(slot, tile, j, tile_next, j_next, state, False, diagonal_next=t + 1 == n_full)
        state = step(0, a, a, b, b, state, True, diagonal_next=True)
        state = step(1, b, b, jnp.minimum(a + 1, n_tiles // 2 - 1), 0, state, True)
        return tuple(st[0] for st in state)

    lax.fori_loop(0, n_tiles // 2, tile_pair, tuple(scores(n_tiles - 1, 0, h, c) for h, c in items))


def _attention(qt, k, vt, lq1, lk1, lq2, lk2, subln_g, lam_init):
    B, H, n_tiles, _, ta = qt.shape
    S = n_tiles * ta
    hs = HEADS_PER_STEP
    vec = lambda n: pl.BlockSpec((1, n), lambda b, h: (0, 0))
    transposed = pl.BlockSpec((1, hs, n_tiles, LANES, ta), lambda b, h: (b, h, 0, 0, 0))
    return pl.pallas_call(
        functools.partial(_attn_kernel, lam_init=lam_init),
        grid=(B, H // hs),
        in_specs=[vec(HEAD_DIM), vec(HEAD_DIM), vec(HEAD_DIM), vec(HEAD_DIM),
                  pl.BlockSpec((V_DIM, 1), lambda b, h: (0, 0)),
                  transposed,
                  pl.BlockSpec((1, hs, S, LANES), lambda b, h: (b, h, 0, 0)),
                  transposed],
        out_specs=pl.BlockSpec((1, S, hs * V_DIM), lambda b, h: (b, 0, h)),
        out_shape=jax.ShapeDtypeStruct((B, S, H * V_DIM), jnp.bfloat16),
        scratch_shapes=[pltpu.VMEM((2, hs, 2 * ta // STRIP, V_DIM, STRIP), jnp.float32),
                        pltpu.VMEM((hs, 2 * ta // STRIP, ta, STRIP), jnp.float32)],
        compiler_params=pltpu.CompilerParams(
            dimension_semantics=("arbitrary", "arbitrary"), vmem_limit_bytes=VMEM_LIMIT),
        name="diff_attn",
    )(lq1, lk1, lq2, lk2, subln_g.reshape(V_DIM, 1), qt, k, vt)


def _mix_mlp_kernel(x_ref, a_ref, gate_ref, up_ref, uprev_ref, wpool_ref, pscale_ref, wout_ref,
                    g2_ref, wup_ref, wdown_ref, gf_ref, o_ref, ext_ref, *, tiles_per_seq, ff_chunk):
    tm = x_ref.shape[0]
    seq_tile = pl.program_id(0) % tiles_per_seq
    bf = jnp.bfloat16

    prev = uprev_ref[...]
    ext_ref[0:POOL_HALO, :] = jnp.where(seq_tile == 0, jnp.zeros_like(prev), prev)
    ext_ref[POOL_HALO:, :] = up_ref[...]
    s_idx = seq_tile * tm + lax.broadcasted_iota(jnp.int32, (tm, 1), 0)
    ys = []
    for g, w in enumerate(POOL_WINDOWS):
        cols = slice(g * POOL_GROUP_IN, (g + 1) * POOL_GROUP_IN)
        u = ext_ref[POOL_HALO:, cols]
        win = u
        for j in range(1, w):
            win = win + ext_ref[POOL_HALO - j:POOL_HALO - j + tm, cols]
        cnt = jnp.minimum(s_idx + 1, w).astype(jnp.float32)
        d = win / cnt - u
        ys.append(jnp.dot(d.astype(bf), wpool_ref[g], preferred_element_type=jnp.float32))
    p = jnp.concatenate(ys, axis=-1) * pscale_ref[...]

    merged = (gate_ref[:, :D_MODEL].astype(jnp.float32) * a_ref[...].astype(jnp.float32)
              + gate_ref[:, D_MODEL:].astype(jnp.float32) * p)
    x1 = x_ref[...] + jnp.dot(merged.astype(bf), wout_ref[...], preferred_element_type=jnp.float32)

    h2 = _rms(x1, g2_ref[...]).astype(bf)
    acc = jnp.zeros((tm, D_MODEL), jnp.float32)
    for c0 in range(0, D_FF, ff_chunk):
        z = jnp.dot(h2, wup_ref[:, c0:c0 + ff_chunk], preferred_element_type=jnp.float32)
        z = jnp.square(jnp.maximum(z, 0.0)).astype(bf)
        acc = acc + jnp.dot(z, wdown_ref[c0:c0 + ff_chunk, :], preferred_element_type=jnp.float32)
    o_ref[...] = _rms(x1 + acc, gf_ref[...])


def _mix_mlp(x2, a2, gates, u_pool, w_pool, pool_scale, w_out, g2, w_up, w_down, gf, S, tm=TOKEN_TILE, ff_chunk=512):
    T = x2.shape[0]
    tiles_per_seq = S // tm
    halo_blocks = tm // POOL_HALO
    const2 = lambda i: (0, 0)
    resident = functools.partial(pl.BlockSpec, pipeline_mode=pl.Buffered(1))
    return pl.pallas_call(
        functools.partial(_mix_mlp_kernel, tiles_per_seq=tiles_per_seq, ff_chunk=ff_chunk),
        grid=(T // tm,),
        in_specs=[
            pl.BlockSpec((tm, D_MODEL), lambda i: (i, 0)),
            pl.BlockSpec((tm, D_MODEL), lambda i: (i, 0)),
            pl.BlockSpec((tm, 2 * D_MODEL), lambda i: (i, 0)),
            pl.BlockSpec((tm, POOL_WIDTH), lambda i: (i, 0)),
            pl.BlockSpec((POOL_HALO, POOL_WIDTH), lambda i: (jnp.maximum(i * halo_blocks - 1, 0), 0)),
            resident((len(POOL_WINDOWS), POOL_GROUP_IN, POOL_GROUP_OUT), lambda i: (0, 0, 0)),
            pl.BlockSpec((1, D_MODEL), const2),
            resident((D_MODEL, D_MODEL), const2),
            pl.BlockSpec((1, D_MODEL), const2),
            resident((D_MODEL, D_FF), const2),
            resident((D_FF, D_MODEL), const2),
            pl.BlockSpec((1, D_MODEL), const2),
        ],
        out_specs=pl.BlockSpec((tm, D_MODEL), lambda i: (i, 0)),
        out_shape=jax.ShapeDtypeStruct((T, D_MODEL), jnp.float32),
        scratch_shapes=[pltpu.VMEM((tm + POOL_HALO, POOL_WIDTH), jnp.float32)],
        compiler_params=pltpu.CompilerParams(
            dimension_semantics=("arbitrary",), vmem_limit_bytes=VMEM_LIMIT),
        name="mix_mlp",
    )(x2, a2, gates, u_pool, u_pool, w_pool, pool_scale, w_out, g2, w_up, w_down, gf)


def kernel(x, positions, norm_attn_g, w_in, lam_q1, lam_k1, lam_q2, lam_k2, subln_g, w_pool, pool_scale,
           w_out, norm_mlp_g, w_up, w_down, final_norm_g):
    B, S, D = x.shape
    assert D == D_MODEL and w_in.shape[0] == 1, "single-layer block of width 1024"
    lam_init = 0.8 - 0.6 * math.exp(-0.3 * 0)
    bf = jnp.bfloat16
    x2 = x.reshape(B * S, D)
    inv_freq = ROPE_THETA ** (-jnp.arange(0, ROPE_DIM, 2, dtype=jnp.float32) / ROPE_DIM)
    invf = inv_freq.reshape(ROPE_DIM // 2, 1)

    qt, k, vt, u_pool, gates = _inproj(x2, positions, invf, norm_attn_g, w_in[0].astype(bf), B, S)
    a = _attention(qt, k, vt, lam_q1, lam_k1, lam_q2, lam_k2, subln_g, lam_init)
    out = _mix_mlp(x2, a.reshape(B * S, D), gates, u_pool, w_pool[0].astype(bf), pool_scale,
                   w_out[0].astype(bf), norm_mlp_g, w_up[0].astype(bf), w_down[0].astype(bf),
                   final_norm_g.reshape(1, D), S)
    return out.reshape(B, S, D)
```
